```python
import jax
import jax.numpy as jnp
from jax import lax
import numpy as np

D_MODEL = 4096
BATCH = 8
SEQ = 2048
DEPTH = 2

N_A_LAYERS = DEPTH // 2
N_B_LAYERS = DEPTH - N_A_LAYERS
DEEPNORM_ALPHA = (2 * DEPTH) ** 0.25
DEEPNORM_BETA = (8 * DEPTH) ** -0.25
LN_EPS = 1e-5

RNN_WIDTH = D_MODEL
RG_BLOCKS = 16
RG_BLOCK_W = RNN_WIDTH // RG_BLOCKS
CONV_WIDTH = 4
RG_C = 8.0

N_HEADS = 32
HEAD_DIM = 128
N_KV_GROUPS = 4
HEADS_PER_GROUP = N_HEADS // N_KV_GROUPS
N_BRANCHES = 3
Q_WIDTH = N_HEADS * HEAD_DIM
KV_WIDTH = 2 * N_BRANCHES * N_KV_GROUPS * HEAD_DIM
CMP_BLOCK = 32
CMP_STRIDE = 16
CMP_HIDDEN = 512
SLC_BLOCK = 64
SLC_TOP_N = 16
SLC_Q_CHUNK = 16
WINDOW = 512
WIN_Q_BLOCK = 128
ROPE_THETA = 10000.0
NEG_INF = -1e30
FORCE_BONUS = 1e4

N_EXPERTS = 32
TOP_K = 4
EXPERT_FF = D_MODEL // 8
SWIGLU_ALPHA = 1.702
SWIGLU_LIMIT = 7.0

kernel_name = 'yoco_rglru_nsa_moe_deepnorm'


def layer_norm(x, g, b):
    xf = x.astype(jnp.float32)
    mu = jnp.mean(xf, axis=-1, keepdims=True)
    var = jnp.mean(jnp.square(xf - mu), axis=-1, keepdims=True)
    y = (xf - mu) * lax.rsqrt(var + LN_EPS)
    return (y * g.astype(jnp.float32) + b.astype(jnp.float32)).astype(x.dtype)


def rope(x, pos):
    half = x.shape[-1] // 2
    inv = ROPE_THETA ** (-jnp.arange(half, dtype=jnp.float32) / half)
    ang = pos.astype(jnp.float32)[:, None] * inv[None, :]
    shape = (pos.shape[0],) + (1,) * (x.ndim - 3) + (half,)
    cos = jnp.cos(ang).reshape(shape).astype(x.dtype)
    sin = jnp.sin(ang).reshape(shape).astype(x.dtype)
    x1, x2 = x[..., :half], x[..., half:]
    return jnp.concatenate([x1 * cos - x2 * sin, x2 * cos + x1 * sin], axis=-1)


def masked_softmax(s, mask):
    p = jax.nn.softmax(jnp.where(mask, s.astype(jnp.float32), NEG_INF), axis=-1)
    return jnp.where(mask, p, 0.0)


def causal_depthwise_conv(u, w, b):
    out = lax.conv_general_dilated(
        u, w[:, None, :].astype(u.dtype), window_strides=(1,),
        padding=[(CONV_WIDTH - 1, 0)], dimension_numbers=('NWC', 'WIO', 'NWC'),
        feature_group_count=u.shape[-1])
    return out + b


def linear_recurrence_combine(left, right):
    a_l, b_l = left
    a_r, b_r = right
    return a_l * a_r, a_r * b_l + b_r


def rglru_mixer(h, w_in, conv_w, conv_b, gate_a_w, gate_a_b, gate_x_w, gate_x_b, lam, w_out):
    B, S, _ = h.shape
    proj = h @ w_in
    y = jax.nn.gelu(proj[..., :RNN_WIDTH])
    u = causal_depthwise_conv(proj[..., RNN_WIDTH:], conv_w, conv_b)
    ub = u.reshape(B, S, RG_BLOCKS, RG_BLOCK_W)
    gate_x = jax.nn.sigmoid(jnp.einsum('bsnc,ncd->bsnd', ub, gate_x_w) + gate_x_b).reshape(B, S, RNN_WIDTH)
    gate_a = jax.nn.sigmoid(jnp.einsum('bsnc,ncd->bsnd', ub, gate_a_w) + gate_a_b).reshape(B, S, RNN_WIDTH)
    log_a = -RG_C * gate_a.astype(jnp.float32) * jax.nn.softplus(-lam.astype(jnp.float32))
    a = jnp.exp(log_a)
    b = jnp.sqrt(-jnp.expm1(2.0 * log_a)) * (gate_x * u).astype(jnp.float32)
    _, hs = lax.associative_scan(linear_recurrence_combine, (a, b), axis=1)
    return (hs.astype(h.dtype) * y) @ w_out


def compress_blocks(k, pe, w1, b1, w2, b2):
    B, S, G, Dh = k.shape
    n_cmp = (S - CMP_BLOCK) // CMP_STRIDE + 1
    idx = np.arange(n_cmp)[:, None] * CMP_STRIDE + np.arange(CMP_BLOCK)[None, :]
    blocks = k[:, idx] + pe[None, None, :, None, :]
    flat = blocks.transpose(0, 1, 3, 2, 4).reshape(B, n_cmp, G, CMP_BLOCK * Dh)
    return jax.nn.gelu(flat @ w1 + b1) @ w2 + b2


def block_overlap(n_cmp, n_slc):
    c0 = np.arange(n_cmp)[:, None] * CMP_STRIDE
    s0 = np.arange(n_slc)[None, :] * SLC_BLOCK
    return ((c0 < s0 + SLC_BLOCK) & (c0 + CMP_BLOCK > s0)).astype(np.float32)


def nsa_shared_kv(h, kv_w, cmp_k_pe, cmp_k_w1, cmp_k_b1, cmp_k_w2, cmp_k_b2,
                  cmp_v_pe, cmp_v_w1, cmp_v_b1, cmp_v_w2, cmp_v_b2):
    B, S, _ = h.shape
    kv = (h @ kv_w).reshape(B, S, 2 * N_BRANCHES, N_KV_GROUPS, HEAD_DIM)
    pos = jnp.arange(S)
    k_cmp = compress_blocks(kv[:, :, 0], cmp_k_pe, cmp_k_w1, cmp_k_b1, cmp_k_w2, cmp_k_b2)
    v_cmp = compress_blocks(kv[:, :, 1], cmp_v_pe, cmp_v_w1, cmp_v_b1, cmp_v_w2, cmp_v_b2)
    n_cmp = k_cmp.shape[1]
    k_cmp = rope(k_cmp, jnp.arange(n_cmp) * CMP_STRIDE + CMP_BLOCK - 1)
    k_slc = rope(kv[:, :, 2], pos)
    v_slc = kv[:, :, 3]
    k_win = rope(kv[:, :, 4], pos)
    v_win = kv[:, :, 5]
    return k_cmp, v_cmp, k_slc, v_slc, k_win, v_win


def selected_attention(q, k, v, sel_idx, scale):
    B, S, G, HG, Dh = q.shape
    n_slc = S // SLC_BLOCK
    n_sel = sel_idx.shape[-1]
    n_ch = S // SLC_Q_CHUNK
    kb = k.reshape(B, n_slc, SLC_BLOCK, G, Dh).transpose(0, 3, 1, 2, 4)
    vb = v.reshape(B, n_slc, SLC_BLOCK, G, Dh).transpose(0, 3, 1, 2, 4)
    qs = q.reshape(B, n_ch, SLC_Q_CHUNK, G, HG, Dh).transpose(1, 0, 2, 3, 4, 5)
    ids = sel_idx.reshape(B, G, n_ch, SLC_Q_CHUNK, n_sel).transpose(2, 0, 1, 3, 4)
    ts = jnp.arange(S).reshape(n_ch, SLC_Q_CHUNK)
    bi = jnp.arange(B)[:, None, None, None]
    gi = jnp.arange(G)[None, :, None, None]
    offs = jnp.arange(SLC_BLOCK)

    def chunk(args):
        qc, ic, tc = args
        kg = kb[bi, gi, ic]
        vg = vb[bi, gi, ic]
        s = jnp.einsum('bqghd,bgqnkd->bghqnk', qc, kg) * scale
        key_pos = ic[..., None] * SLC_BLOCK + offs
        mask = key_pos <= tc[:, None, None]
        p = masked_softmax(s.reshape(B, G, HG, SLC_Q_CHUNK, n_sel * SLC_BLOCK),
                           mask.reshape(B, G, 1, SLC_Q_CHUNK, n_sel * SLC_BLOCK))
        p = p.reshape(B, G, HG, SLC_Q_CHUNK, n_sel, SLC_BLOCK).astype(vg.dtype)
        return jnp.einsum('bghqnk,bgqnkd->bqghd', p, vg)

    o = lax.map(chunk, (qs, ids, ts))
    return o.transpose(1, 0, 2, 3, 4, 5).reshape(B, S, G, HG, Dh)


def window_attention(q, k, v, scale):
    B, S, G, HG, Dh = q.shape
    n_blk = S // WIN_Q_BLOCK
    span = WINDOW + WIN_Q_BLOCK
    kp = jnp.pad(k, ((0, 0), (WINDOW, 0), (0, 0), (0, 0)))
    vp = jnp.pad(v, ((0, 0), (WINDOW, 0), (0, 0), (0, 0)))
    qb = q.reshape(B, n_blk, WIN_Q_BLOCK, G, HG, Dh).transpose(1, 0, 2, 3, 4, 5)
    starts = jnp.arange(n_blk) * WIN_Q_BLOCK

    def block(args):
        qblk, s0 = args
        kblk = lax.dynamic_slice_in_dim(kp, s0, span, axis=1)
        vblk = lax.dynamic_slice_in_dim(vp, s0, span, axis=1)
        s = jnp.einsum('bqghd,bkgd->bghqk', qblk, kblk) * scale
        tq = s0 + jnp.arange(WIN_Q_BLOCK)
        tk = s0 - WINDOW + jnp.arange(span)
        dist = tq[:, None] - tk[None, :]
        mask = (dist >= 0) & (dist < WINDOW) & (tk[None, :] >= 0)
        p = masked_softmax(s, mask).astype(vblk.dtype)
        return jnp.einsum('bghqk,bkgd->bqghd', p, vblk)

    o = lax.map(block, (qb, starts))
    return o.transpose(1, 0, 2, 3, 4, 5).reshape(B, S, G, HG, Dh)


def nsa_mixer(h, shared, w_in, w_out):
    k_cmp, v_cmp, k_slc, v_slc, k_win, v_win = shared
    B, S, _ = h.shape
    G, HG, Dh = N_KV_GROUPS, HEADS_PER_GROUP, HEAD_DIM
    scale = HEAD_DIM ** -0.5
    pos = jnp.arange(S)
    proj = h @ w_in
    q = rope(proj[..., :Q_WIDTH].reshape(B, S, G, HG, Dh), pos)
    gates = jax.nn.sigmoid(proj[..., Q_WIDTH:].astype(jnp.float32)).reshape(B, S, G, HG, N_BRANCHES).astype(h.dtype)

    n_cmp = k_cmp.shape[1]
    s_cmp = jnp.einsum('bsghd,bcgd->bghsc', q, k_cmp) * scale
    cmp_end = jnp.arange(n_cmp) * CMP_STRIDE + CMP_BLOCK - 1
    p_cmp = masked_softmax(s_cmp, cmp_end[None, :] <= pos[:, None])
    o_cmp = jnp.einsum('bghsc,bcgd->bsghd', p_cmp.astype(v_cmp.dtype), v_cmp)

    n_slc = S // SLC_BLOCK
    n_sel = min(SLC_TOP_N, n_slc)
    overlap = jnp.asarray(block_overlap(n_cmp, n_slc))
    imp = jnp.einsum('bghsc,cn->bgsn', p_cmp, overlap)
    blk = jnp.arange(n_slc)[None, :]
    cur = (pos // SLC_BLOCK)[:, None]
    causal_blk = blk * SLC_BLOCK <= pos[:, None]
    forced = (blk == 0) | (blk == cur) | (blk == cur - 1)
    imp = jnp.where(causal_blk, imp + jnp.where(forced, FORCE_BONUS, 0.0), -jnp.inf)
    _, sel_idx = lax.top_k(imp, n_sel)
    o_slc = selected_attention(q, k_slc, v_slc, sel_idx, scale)

    o_win = window_attention(q, k_win, v_win, scale)

    o = gates[..., 0:1] * o_cmp + gates[..., 1:2] * o_slc + gates[..., 2:3] * o_win
    return o.reshape(B, S, Q_WIDTH) @ w_out


def moe_ffn(h, router_w, router_b, w_gu, b_gu, w_down, b_down):
    B, S, D = h.shape
    xt = h.reshape(B * S, D)
    logits = (xt @ router_w + router_b).astype(jnp.float32)
    top_v, top_i = lax.top_k(logits, TOP_K)
    top_w = jax.nn.softmax(top_v, axis=-1)
    combine = jnp.sum(jax.nn.one_hot(top_i, N_EXPERTS, dtype=jnp.float32) * top_w[..., None], axis=1)

    def expert(acc, p):
        wgu, bgu, wd, bd, c = p
        gu = xt @ wgu + bgu
        gate = jnp.minimum(gu[:, :EXPERT_FF], SWIGLU_LIMIT)
        up = jnp.clip(gu[:, EXPERT_FF:], -SWIGLU_LIMIT, SWIGLU_LIMIT)
        act = (up + 1.0) * gate * jax.nn.sigmoid(SWIGLU_ALPHA * gate)
        return acc + c[:, None] * (act @ wd + bd), None

    out, _ = lax.scan(expert, jnp.zeros_like(xt),
                      (w_gu, b_gu, w_down, b_down, combine.T.astype(h.dtype)))
    return out.reshape(B, S, D)


def setup_inputs(seed: int = 0) -> dict:
    key = jax.random.key(seed)
    ks = iter(jax.random.split(key, 48))

    def nrm(shape, scale):
        return jax.random.normal(next(ks), shape, jnp.float32) * scale

    D, DR, E, F = D_MODEL, RNN_WIDTH, N_EXPERTS, EXPERT_FF
    NA, NB = N_A_LAYERS, N_B_LAYERS
    x = nrm((BATCH, SEQ, D), 1.0)
    ln_mix_g = 1.0 + nrm((DEPTH, D), 0.02)
    ln_mix_b = nrm((DEPTH, D), 0.02)
    ln_ffn_g = 1.0 + nrm((DEPTH, D), 0.02)
    ln_ffn_b = nrm((DEPTH, D), 0.02)
    a_w_in = nrm((NA, D, 2 * DR), D ** -0.5)
    a_conv_w = nrm((NA, CONV_WIDTH, DR), CONV_WIDTH ** -0.5)
    a_conv_b = nrm((NA, DR), 0.02)
    a_gate_a_w = nrm((NA, RG_BLOCKS, RG_BLOCK_W, RG_BLOCK_W), RG_BLOCK_W ** -0.5)
    a_gate_a_b = nrm((NA, RG_BLOCKS, RG_BLOCK_W), 0.02)
    a_gate_x_w = nrm((NA, RG_BLOCKS, RG_BLOCK_W, RG_BLOCK_W), RG_BLOCK_W ** -0.5)
    a_gate_x_b = nrm((NA, RG_BLOCKS, RG_BLOCK_W), 0.02)
    u = jax.random.uniform(next(ks), (NA, DR), jnp.float32, minval=0.9, maxval=0.999)
    a_lambda = jnp.log(u) - jnp.log1p(-u)
    a_w_out = nrm((NA, DR, D), DEEPNORM_BETA * DR ** -0.5)
    b_w_in = nrm((NB, D, Q_WIDTH + N_BRANCHES * N_HEADS), D ** -0.5)
    b_w_out = nrm((NB, Q_WIDTH, D), DEEPNORM_BETA * Q_WIDTH ** -0.5)
    kv_w = nrm((D, KV_WIDTH), D ** -0.5)
    cmp_k_pe = nrm((CMP_BLOCK, HEAD_DIM), 0.5)
    cmp_k_w1 = nrm((CMP_BLOCK * HEAD_DIM, CMP_HIDDEN), (CMP_BLOCK * HEAD_DIM) ** -0.5)
    cmp_k_b1 = nrm((CMP_HIDDEN,), 0.02)
    cmp_k_w2 = nrm((CMP_HIDDEN, HEAD_DIM), CMP_HIDDEN ** -0.5)
    cmp_k_b2 = nrm((HEAD_DIM,), 0.02)
    cmp_v_pe = nrm((CMP_BLOCK, HEAD_DIM), 0.5)
    cmp_v_w1 = nrm((CMP_BLOCK * HEAD_DIM, CMP_HIDDEN), (CMP_BLOCK * HEAD_DIM) ** -0.5)
    cmp_v_b1 = nrm((CMP_HIDDEN,), 0.02)
    cmp_v_w2 = nrm((CMP_HIDDEN, HEAD_DIM), CMP_HIDDEN ** -0.5)
    cmp_v_b2 = nrm((HEAD_DIM,), 0.02)
    moe_router_w = nrm((DEPTH, D, E), D ** -0.5)
    moe_router_b = nrm((DEPTH, E), 0.01)
    moe_w_gu = nrm((DEPTH, E, D, 2 * F), D ** -0.5)
    moe_b_gu = nrm((DEPTH, E, 2 * F), 0.02)
    moe_w_down = nrm((DEPTH, E, F, D), DEEPNORM_BETA * F ** -0.5)
    moe_b_down = nrm((DEPTH, E, D), 0.02)
    return {'x': x, 'ln_mix_g': ln_mix_g, 'ln_mix_b': ln_mix_b, 'ln_ffn_g': ln_ffn_g, 'ln_ffn_b': ln_ffn_b,
            'a_w_in': a_w_in, 'a_conv_w': a_conv_w, 'a_conv_b': a_conv_b,
            'a_gate_a_w': a_gate_a_w, 'a_gate_a_b': a_gate_a_b, 'a_gate_x_w': a_gate_x_w, 'a_gate_x_b': a_gate_x_b,
            'a_lambda': a_lambda, 'a_w_out': a_w_out, 'b_w_in': b_w_in, 'b_w_out': b_w_out, 'kv_w': kv_w,
            'cmp_k_pe': cmp_k_pe, 'cmp_k_w1': cmp_k_w1, 'cmp_k_b1': cmp_k_b1, 'cmp_k_w2': cmp_k_w2, 'cmp_k_b2': cmp_k_b2,
            'cmp_v_pe': cmp_v_pe, 'cmp_v_w1': cmp_v_w1, 'cmp_v_b1': cmp_v_b1, 'cmp_v_w2': cmp_v_w2, 'cmp_v_b2': cmp_v_b2,
            'moe_router_w': moe_router_w, 'moe_router_b': moe_router_b, 'moe_w_gu': moe_w_gu, 'moe_b_gu': moe_b_gu,
            'moe_w_down': moe_w_down, 'moe_b_down': moe_b_down}


def reference(x, ln_mix_g, ln_mix_b, ln_ffn_g, ln_ffn_b,
              a_w_in, a_conv_w, a_conv_b, a_gate_a_w, a_gate_a_b, a_gate_x_w, a_gate_x_b, a_lambda, a_w_out,
              b_w_in, b_w_out, kv_w,
              cmp_k_pe, cmp_k_w1, cmp_k_b1, cmp_k_w2, cmp_k_b2,
              cmp_v_pe, cmp_v_w1, cmp_v_b1, cmp_v_w2, cmp_v_b2,
              moe_router_w, moe_router_b, moe_w_gu, moe_b_gu, moe_w_down, moe_b_down):
    h = x
    shared = None
    for layer in range(DEPTH):
        if layer < N_A_LAYERS:
            i = layer
            mix = rglru_mixer(h, a_w_in[i], a_conv_w[i], a_conv_b[i], a_gate_a_w[i], a_gate_a_b[i],
                              a_gate_x_w[i], a_gate_x_b[i], a_lambda[i], a_w_out[i])
        else:
            if layer == N_A_LAYERS:
                shared = nsa_shared_kv(h, kv_w, cmp_k_pe, cmp_k_w1, cmp_k_b1, cmp_k_w2, cmp_k_b2,
                                       cmp_v_pe, cmp_v_w1, cmp_v_b1, cmp_v_w2, cmp_v_b2)
            j = layer - N_A_LAYERS
            mix = nsa_mixer(h, shared, b_w_in[j], b_w_out[j])
        h = layer_norm(DEEPNORM_ALPHA * h + mix, ln_mix_g[layer], ln_mix_b[layer])
        ffn = moe_ffn(h, moe_router_w[layer], moe_router_b[layer], moe_w_gu[layer], moe_b_gu[layer],
                      moe_w_down[layer], moe_b_down[layer])
        h = layer_norm(DEEPNORM_ALPHA * h + ffn, ln_ffn_g[layer], ln_ffn_b[layer])
    return h
```

```python
import functools
import math

import jax
import jax.numpy as jnp
from jax import lax
from jax.experimental import pallas as pl
from jax.experimental.pallas import tpu as pltpu

F32 = jnp.float32
BF16 = jnp.bfloat16

D_MODEL = 4096
SEQ = 2048
DEPTH = 2
DEEPNORM_ALPHA = (2 * DEPTH) ** 0.25
LN_EPS = 1e-5
RNN_WIDTH = D_MODEL
RG_BLOCKS = 16
RG_BLOCK_W = RNN_WIDTH // RG_BLOCKS
CONV_WIDTH = 4
RG_C = 8.0
N_HEADS = 32
HEAD_DIM = 128
N_KV_GROUPS = 4
HEADS_PER_GROUP = N_HEADS // N_KV_GROUPS
N_BRANCHES = 3
Q_WIDTH = N_HEADS * HEAD_DIM
KV_SLOT_W = N_KV_GROUPS * HEAD_DIM
KV_WIDTH = 2 * N_BRANCHES * KV_SLOT_W
CMP_BLOCK = 32
CMP_STRIDE = 16
CMP_HIDDEN = 512
N_CMP = (SEQ - CMP_BLOCK) // CMP_STRIDE + 1
N_CHUNK = SEQ // CMP_STRIDE
SLC_BLOCK = 64
N_SLC = SEQ // SLC_BLOCK
SLC_TOP_N = min(16, N_SLC)
WINDOW = 512
ROPE_THETA = 10000.0
NEG_BIG = -1e30
FORCE_BONUS = 1e4
N_EXPERTS = 32
TOP_K = 4
EXPERT_FF = D_MODEL // 8
SWIGLU_ALPHA = 1.702
SWIGLU_LIMIT = 7.0

LANES = 128
SUBLANES = 8
ROW_CHUNKS = D_MODEL // LANES
VMEM_LIMIT = 56 * 1024 * 1024

MM_BM = 1024
MM_BN = 1024
LN_ROWS = 256
COMBINE_ROWS = 128
RANK_ROWS = 512
MOE_TM = 256
ATT_TQ = 256


def _params(sem, vmem=VMEM_LIMIT):
    return pltpu.CompilerParams(dimension_semantics=sem, vmem_limit_bytes=vmem)


def _mm_body(x_ref, w_ref, o_ref):
    o_ref[...] = jnp.dot(x_ref[...], w_ref[...], preferred_element_type=F32).astype(o_ref.dtype)


def _matmul(x, w, out_dtype, name, bm=MM_BM, bn=MM_BN):
    m, k = x.shape
    n = w.shape[1]
    return pl.pallas_call(
        _mm_body,
        grid=(n // bn, m // bm),
        in_specs=[pl.BlockSpec((bm, k), lambda j, i: (i, 0)),
                  pl.BlockSpec((k, bn), lambda j, i: (0, j))],
        out_specs=pl.BlockSpec((bm, bn), lambda j, i: (i, j)),
        out_shape=jax.ShapeDtypeStruct((m, n), out_dtype),
        compiler_params=_params(("arbitrary", "arbitrary")),
        name=name,
    )(x, w)


def _mm_rope_body(x_ref, w_ref, cos_ref, sin_ref, o_ref, *, scale, rope_cols):
    acc = jnp.dot(x_ref[...], w_ref[...], preferred_element_type=F32)
    cos = cos_ref[...]
    sin = sin_ref[...]
    if rope_cols is not None:
        j = pl.program_id(0)
        flag = functools.reduce(jnp.logical_or, [j == c for c in rope_cols])
        cos = jnp.where(flag, cos, 1.0)
        sin = jnp.where(flag, sin, 0.0)
    for h in range(acc.shape[1] // HEAD_DIM):
        a = acc[:, h * HEAD_DIM:(h + 1) * HEAD_DIM]
        r = a * cos + pltpu.roll(a, HEAD_DIM // 2, axis=1) * sin
        o_ref[:, h * HEAD_DIM:(h + 1) * HEAD_DIM] = (r * scale).astype(o_ref.dtype)


def _matmul_rope(x, w, cos, sin, out_dtype, name, *, scale, rope_cols, bm, bn):
    m, k = x.shape
    n = w.shape[1]
    s_blocks = SEQ // bm
    return pl.pallas_call(
        functools.partial(_mm_rope_body, scale=scale, rope_cols=rope_cols),
        grid=(n // bn, m // bm),
        in_specs=[pl.BlockSpec((bm, k), lambda j, i: (i, 0)),
                  pl.BlockSpec((k, bn), lambda j, i: (0, j)),
                  pl.BlockSpec((bm, HEAD_DIM), lambda j, i: (i % s_blocks, 0)),
                  pl.BlockSpec((bm, HEAD_DIM), lambda j, i: (i % s_blocks, 0))],
        out_specs=pl.BlockSpec((bm, bn), lambda j, i: (i, j)),
        out_shape=jax.ShapeDtypeStruct((m, n), out_dtype),
        compiler_params=_params(("arbitrary", "arbitrary")),
        name=name,
    )(x, w, cos, sin)


def _rope_tables(pos):
    half = HEAD_DIM // 2
    inv = ROPE_THETA ** (-jnp.arange(half, dtype=F32) / half)
    ang = pos.astype(F32)[:, None] * inv[None, :]
    cos = jnp.cos(ang)
    sin = jnp.sin(ang)
    return jnp.concatenate([cos, cos], axis=1), jnp.concatenate([-sin, sin], axis=1)


def _rglru_body(y_ref, u_ref, cw_ref, cb_ref, wa_ref, ba_ref, wx_ref, bx_ref, lam_ref, o_ref,
                a_scr, b_scr):
    s = u_ref.shape[0]
    u_in = u_ref[...].astype(F32)
    row = lax.broadcasted_iota(jnp.int32, u_in.shape, 0)
    u = u_in * cw_ref[CONV_WIDTH - 1:CONV_WIDTH, :] + cb_ref[...]
    for d in range(1, CONV_WIDTH):
        shifted = jnp.where(row >= d, pltpu.roll(u_in, d, axis=0), 0.0)
        u = u + shifted * cw_ref[CONV_WIDTH - 1 - d:CONV_WIDTH - d, :]
    ub = u.astype(BF16)
    gate_x = jax.nn.sigmoid(jnp.dot(ub, wx_ref[0], preferred_element_type=F32) + bx_ref[0])
    gate_a = jax.nn.sigmoid(jnp.dot(ub, wa_ref[0], preferred_element_type=F32) + ba_ref[0])
    z = -lam_ref[...]
    softplus = jnp.maximum(z, 0.0) + jnp.log1p(jnp.exp(-jnp.abs(z)))
    log_a = -RG_C * gate_a * softplus
    a = jnp.exp(log_a)
    th = jnp.tanh(log_a)
    b = jnp.sqrt(-2.0 * th / (1.0 - th)) * (gate_x * u)
    sub = row % SUBLANES
    for d in (1, 2, 4):
        valid = sub >= d
        a_prev = pltpu.roll(a, d, axis=0)
        b_prev = pltpu.roll(b, d, axis=0)
        b = jnp.where(valid, a * b_prev + b, b)
        a = jnp.where(valid, a * a_prev, a)
    a_scr[...] = a
    b_scr[...] = b

    def step(g, carry):
        r0 = pl.multiple_of(g * SUBLANES, SUBLANES)
        hs = a_scr[pl.ds(r0, SUBLANES), :] * carry + b_scr[pl.ds(r0, SUBLANES), :]
        b_scr[pl.ds(r0, SUBLANES), :] = hs
        return jnp.broadcast_to(hs[SUBLANES - 1:SUBLANES, :], hs.shape)

    lax.fori_loop(0, s // SUBLANES, step, jnp.zeros((SUBLANES, u_in.shape[1]), F32))
    y = jax.nn.gelu(y_ref[...].astype(F32))
    o_ref[...] = (b_scr[...] * y).astype(o_ref.dtype)


def _rglru(proj, conv_w, conv_b, gate_a_w, gate_a_b, gate_x_w, gate_x_b, lam, batch):
    w = RG_BLOCK_W
    t = batch * SEQ
    vec = lambda b, n: (0, n)
    blk = lambda b, n: (n, 0, 0)
    return pl.pallas_call(
        _rglru_body,
        grid=(batch, RG_BLOCKS),
        in_specs=[pl.BlockSpec((SEQ, w), lambda b, n: (b, n)),
                  pl.BlockSpec((SEQ, w), lambda b, n: (b, RG_BLOCKS + n)),
                  pl.BlockSpec((CONV_WIDTH, w), vec),
                  pl.BlockSpec((1, w), vec),
                  pl.BlockSpec((1, w, w), blk), pl.BlockSpec((1, 1, w), blk),
                  pl.BlockSpec((1, w, w), blk), pl.BlockSpec((1, 1, w), blk),
                  pl.BlockSpec((1, w), vec)],
        out_specs=pl.BlockSpec((SEQ, w), lambda b, n: (b, n)),
        out_shape=jax.ShapeDtypeStruct((t, RNN_WIDTH), BF16),
        scratch_shapes=[pltpu.VMEM((SEQ, w), F32), pltpu.VMEM((SEQ, w), F32)],
        compiler_params=_params(("arbitrary", "arbitrary")),
        name="rglru",
    )(proj, proj, conv_w, conv_b.reshape(1, -1),
      gate_a_w, gate_a_b.reshape(RG_BLOCKS, 1, w), gate_x_w, gate_x_b.reshape(RG_BLOCKS, 1, w),
      lam.reshape(1, -1))


def _split_hi_lo(w):
    hi = w.astype(BF16)
    lo = (w - hi.astype(F32)).astype(BF16)
    return hi, lo


def _dot3(x, whi_ref, wlo_ref):
    xh = x.astype(BF16)
    xl = (x - xh.astype(F32)).astype(BF16)
    whi = whi_ref[...]
    return (jnp.dot(xh, whi, preferred_element_type=F32)
            + jnp.dot(xl, whi, preferred_element_type=F32)
            + jnp.dot(xh, wlo_ref[...], preferred_element_type=F32))


def _layer_norm(z, g, b):
    mu = jnp.mean(z, axis=-1, keepdims=True)
    zc = z - mu
    var = jnp.mean(zc * zc, axis=-1, keepdims=True)
    return zc * lax.rsqrt(var + LN_EPS) * g + b


def _lane_pack(cols, lane):
    out = jnp.broadcast_to(cols[-1], lane.shape)
    for k in range(len(cols) - 2, -1, -1):
        out = jnp.where(lane == k, cols[k], out)
    return out


def _ln_router_body(res_ref, mix_ref, g_ref, b_ref, whi_ref, wlo_ref, rb_ref,
                    hrow_ref, ids_ref, wts_ref, cnt_ref):
    rows = res_ref.shape[0]
    z = DEEPNORM_ALPHA * res_ref[...] + mix_ref[...].astype(F32)
    h = _layer_norm(z, g_ref[...], b_ref[...])
    for j in range(ROW_CHUNKS):
        hrow_ref[pl.ds(j, rows, stride=ROW_CHUNKS), :] = h[:, j * LANES:(j + 1) * LANES]
    logits = _dot3(h, whi_ref, wlo_ref) + rb_ref[...]
    lane = lax.broadcasted_iota(jnp.int32, logits.shape, 1)
    cur = jnp.where(lane < N_EXPERTS, logits, -jnp.inf)
    vals, idxs = [], []
    for _ in range(TOP_K):
        m = jnp.max(cur, axis=-1, keepdims=True)
        idx = jnp.min(jnp.where(cur == m, lane, LANES), axis=-1, keepdims=True)
        vals.append(m)
        idxs.append(idx)
        cur = jnp.where(lane == idx, -jnp.inf, cur)
    es = [jnp.exp(v - vals[0]) for v in vals]
    tot = functools.reduce(lambda p, q: p + q, es)
    ids_ref[...] = _lane_pack(idxs, lane)
    wts_ref[...] = _lane_pack([e / tot for e in es], lane)
    hot = functools.reduce(lambda p, q: p + q, [(lane == i).astype(F32) for i in idxs])

    @pl.when(pl.program_id(0) == 0)
    def _():
        cnt_ref[...] = jnp.zeros_like(cnt_ref)

    cnt_ref[...] += jnp.broadcast_to(jnp.sum(hot, axis=0, keepdims=True), cnt_ref.shape)


def _ln_router(res, mix, g, b, router_w, router_b):
    t = res.shape[0]
    rows = LN_ROWS
    w_pad = jnp.pad(router_w, ((0, 0), (0, LANES - N_EXPERTS)))
    whi, wlo = _split_hi_lo(w_pad)
    rb = jnp.pad(router_b, (0, LANES - N_EXPERTS)).reshape(1, LANES)
    const = lambda i: (0, 0)
    return pl.pallas_call(
        _ln_router_body,
        grid=(t // rows,),
        in_specs=[pl.BlockSpec((rows, D_MODEL), lambda i: (i, 0)),
                  pl.BlockSpec((rows, D_MODEL), lambda i: (i, 0)),
                  pl.BlockSpec((1, D_MODEL), const), pl.BlockSpec((1, D_MODEL), const),
                  pl.BlockSpec((D_MODEL, LANES), const), pl.BlockSpec((D_MODEL, LANES), const),
                  pl.BlockSpec((1, LANES), const)],
        out_specs=[pl.BlockSpec((rows * ROW_CHUNKS, LANES), lambda i: (i, 0)),
                   pl.BlockSpec((rows, LANES), lambda i: (i, 0)),
                   pl.BlockSpec((rows, LANES), lambda i: (i, 0)),
                   pl.BlockSpec((SUBLANES, LANES), const)],
        out_shape=[jax.ShapeDtypeStruct((t * ROW_CHUNKS, LANES), F32),
                   jax.ShapeDtypeStruct((t, LANES), jnp.int32),
                   jax.ShapeDtypeStruct((t, LANES), F32),
                   jax.ShapeDtypeStruct((SUBLANES, LANES), F32)],
        compiler_params=_params(("arbitrary",)),
        name="ln_router",
    )(res, mix, g.reshape(1, -1), b.reshape(1, -1), whi, wlo, rb)


def _moe_tiles(t):
    return (t * TOP_K) // MOE_TM + N_EXPERTS


def _rank_body(ids_ref, cnt_ref, pos_ref, te_ref, meta_ref, carry_scr, base_scr):
    rows = ids_ref.shape[0]
    lane8 = lax.broadcasted_iota(jnp.int32, (SUBLANES, LANES), 1)

    @pl.when(pl.program_id(0) == 0)
    def _():
        cnt = cnt_ref[...]
        padded = jnp.floor((cnt + (MOE_TM - 1)) / MOE_TM) * MOE_TM
        padded = jnp.where(lane8 < N_EXPERTS, padded, 0.0)
        ends = padded
        d = 1
        while d < N_EXPERTS:
            ends = ends + jnp.where(lane8 >= d, pltpu.roll(ends, d, axis=1), 0.0)
            d *= 2
        base = ends - padded
        base_scr[...] = base
        carry_scr[...] = jnp.zeros_like(carry_scr)
        sub8 = lax.broadcasted_iota(jnp.int32, (SUBLANES, LANES), 0)
        meta = jnp.where(sub8 == 0, base, jnp.where(sub8 == 1, cnt, jnp.where(sub8 == 2, padded, ends)))
        meta_ref[...] = meta.astype(jnp.int32)
        tstart = lax.broadcasted_iota(jnp.int32, te_ref.shape, 0).astype(F32) * MOE_TM
        lane_t = lax.broadcasted_iota(jnp.int32, te_ref.shape, 1)
        done = jnp.where(lane_t < N_EXPERTS, (ends[0:1, :] <= tstart).astype(F32), 0.0)
        te = jnp.minimum(jnp.sum(done, axis=-1, keepdims=True), N_EXPERTS - 1.0)
        te_ref[...] = jnp.broadcast_to(te, te_ref.shape).astype(jnp.int32)

    ids = ids_ref[...]
    lane = lax.broadcasted_iota(jnp.int32, ids.shape, 1)
    hits = [ids[:, k:k + 1] == lane for k in range(TOP_K)]
    hot = functools.reduce(lambda p, q: p + q, [h.astype(F32) for h in hits])
    r_i = lax.broadcasted_iota(jnp.int32, (rows, rows), 0)
    c_i = lax.broadcasted_iota(jnp.int32, (rows, rows), 1)
    lower = jnp.where(r_i > c_i, 1.0, 0.0).astype(BF16)
    before = jnp.dot(lower, hot.astype(BF16), preferred_element_type=F32)
    slot = before + carry_scr[0:1, :] + base_scr[0:1, :]
    cols = [jnp.sum(jnp.where(h, slot, 0.0), axis=-1, keepdims=True) for h in hits]
    pos_ref[...] = _lane_pack(cols, lane).astype(jnp.int32)
    carry_scr[...] += jnp.broadcast_to(jnp.sum(hot, axis=0, keepdims=True), carry_scr.shape)


def _rank(ids, cnt):
    t = ids.shape[0]
    rows = RANK_ROWS
    nt_pad = -(-_moe_tiles(t) // SUBLANES) * SUBLANES
    const = lambda i: (0, 0)
    return pl.pallas_call(
        _rank_body,
        grid=(t // rows,),
        in_specs=[pl.BlockSpec((rows, LANES), lambda i: (i, 0)),
                  pl.BlockSpec((SUBLANES, LANES), const)],
        out_specs=[pl.BlockSpec((rows, LANES), lambda i: (i, 0)),
                   pl.BlockSpec((nt_pad, LANES), const),
                   pl.BlockSpec((SUBLANES, LANES), const)],
        out_shape=[jax.ShapeDtypeStruct((t, LANES), jnp.int32),
                   jax.ShapeDtypeStruct((nt_pad, LANES), jnp.int32),
                   jax.ShapeDtypeStruct((SUBLANES, LANES), jnp.int32)],
        scratch_shapes=[pltpu.VMEM((SUBLANES, LANES), F32), pltpu.VMEM((SUBLANES, LANES), F32)],
        compiler_params=_params(("arbitrary",)),
        name="moe_rank",
    )(ids, cnt)


def _dispatch_body(pos_ref, base_ref, cnt_ref, pcnt_ref, h_ref, xs_ref, zero_scr, sem):
    i = pl.program_id(0)
    rows = h_ref.shape[0] // ROW_CHUNKS

    def row_copy(src_row_ref, slot):
        dst = pl.multiple_of(slot * ROW_CHUNKS, ROW_CHUNKS)
        return pltpu.make_async_copy(src_row_ref, xs_ref.at[pl.ds(dst, ROW_CHUNKS)], sem)

    @pl.when(i == 0)
    def _():
        zero_scr[...] = jnp.zeros_like(zero_scr)

        def per_expert(e, _):
            def start(r, _):
                row_copy(zero_scr, base_ref[e] + r).start()
                return 0

            def wait(r, _):
                row_copy(zero_scr, base_ref[e] + r).wait()
                return 0

            lax.fori_loop(cnt_ref[e], pcnt_ref[e], start, 0)
            lax.fori_loop(cnt_ref[e], pcnt_ref[e], wait, 0)
            return 0

        lax.fori_loop(0, N_EXPERTS, per_expert, 0)

    def src(t):
        return h_ref.at[pl.ds(pl.multiple_of(t * ROW_CHUNKS, ROW_CHUNKS), ROW_CHUNKS)]

    def start(t, _):
        for k in range(TOP_K):
            row_copy(src(t), pos_ref[(i * rows + t) * TOP_K + k]).start()
        return 0

    def wait(t, _):
        for k in range(TOP_K):
            row_copy(src(t), pos_ref[(i * rows + t) * TOP_K + k]).wait()
        return 0

    lax.fori_loop(0, rows, start, 0)
    lax.fori_loop(0, rows, wait, 0)


def _dispatch(h_rows, pos, base, cnt, pcnt):
    t = h_rows.shape[0] // ROW_CHUNKS
    rows = LN_ROWS
    p_pad = _moe_tiles(t) * MOE_TM
    return pl.pallas_call(
        _dispatch_body,
        grid_spec=pltpu.PrefetchScalarGridSpec(
            num_scalar_prefetch=4,
            grid=(t // rows,),
            in_specs=[pl.BlockSpec((rows * ROW_CHUNKS, LANES), lambda i, *_: (i, 0))],
            out_specs=pl.BlockSpec(memory_space=pl.ANY),
            scratch_shapes=[pltpu.VMEM((ROW_CHUNKS, LANES), F32), pltpu.SemaphoreType.DMA(())]),
        out_shape=jax.ShapeDtypeStruct((p_pad * ROW_CHUNKS, LANES), F32),
        compiler_params=_params(("arbitrary",)),
        name="moe_dispatch",
    )(pos, base, cnt, pcnt, h_rows)


def _ffn_body(te_ref, nt_ref, xs_ref, wgu_ref, bgu_ref, wd_ref, bd_ref, y_ref, x_scr):
    rows = x_scr.shape[0]

    @pl.when(pl.program_id(0) < nt_ref[0])
    def _():
        for j in range(ROW_CHUNKS):
            x_scr[:, j * LANES:(j + 1) * LANES] = xs_ref[pl.ds(j, rows, stride=ROW_CHUNKS), :].astype(BF16)
        gu = jnp.dot(x_scr[...], wgu_ref[0], preferred_element_type=F32) + bgu_ref[0]
        gate = jnp.minimum(gu[:, :EXPERT_FF], SWIGLU_LIMIT)
        up = jnp.clip(gu[:, EXPERT_FF:], -SWIGLU_LIMIT, SWIGLU_LIMIT)
        act = (up + 1.0) * gate * jax.nn.sigmoid(SWIGLU_ALPHA * gate)
        y = jnp.dot(act.astype(BF16), wd_ref[0], preferred_element_type=F32) + bd_ref[0]
        for j in range(ROW_CHUNKS):
            y_ref[pl.ds(j, rows, stride=ROW_CHUNKS), :] = y[:, j * LANES:(j + 1) * LANES]

    @pl.when(pl.program_id(0) >= nt_ref[0])
    def _():
        y_ref[...] = jnp.zeros_like(y_ref)


def _grouped_ffn(xs, tile_expert, n_tiles, w_gu, b_gu, w_down, b_down):
    p_pad = xs.shape[0] // ROW_CHUNKS
    tm = MOE_TM
    row_blk = lambda i, te, nt: (jnp.minimum(i, nt[0] - 1), 0)
    exp_blk = lambda i, te, nt: (te[i], 0, 0)
    return pl.pallas_call(
        _ffn_body,
        grid_spec=pltpu.PrefetchScalarGridSpec(
            num_scalar_prefetch=2,
            grid=(p_pad // tm,),
            in_specs=[pl.BlockSpec((tm * ROW_CHUNKS, LANES), row_blk),
                      pl.BlockSpec((1, D_MODEL, 2 * EXPERT_FF), exp_blk),
                      pl.BlockSpec((1, 1, 2 * EXPERT_FF), exp_blk),
                      pl.BlockSpec((1, EXPERT_FF, D_MODEL), exp_blk),
                      pl.BlockSpec((1, 1, D_MODEL), exp_blk)],
            out_specs=pl.BlockSpec((tm * ROW_CHUNKS, LANES), lambda i, te, nt: (i, 0)),
            scratch_shapes=[pltpu.VMEM((tm, D_MODEL), BF16)]),
        out_shape=jax.ShapeDtypeStruct(xs.shape, F32),
        compiler_params=_params(("arbitrary",)),
        name="moe_ffn",
    )(tile_expert, n_tiles, xs, w_gu, b_gu.reshape(N_EXPERTS, 1, -1), w_down,
      b_down.reshape(N_EXPERTS, 1, -1))


def _combine_ln_body(pos_ref, hrow_ref, wts_ref, g_ref, b_ref, *rest, n_side):
    side_refs = rest[:3 * n_side]
    y_hbm = rest[3 * n_side]
    outs = rest[3 * n_side + 1:-3]
    ybuf, z_scr, sem = rest[-3:]
    h_ref, hb_ref = outs[0], outs[1]
    i = pl.program_id(0)
    rows = wts_ref.shape[0]

    def row_copy(t, k):
        slot = pos_ref[(i * rows + t) * TOP_K + k]
        src = y_hbm.at[pl.ds(pl.multiple_of(slot * ROW_CHUNKS, ROW_CHUNKS), ROW_CHUNKS)]
        dst = ybuf.at[k, pl.ds(pl.multiple_of(t * ROW_CHUNKS, ROW_CHUNKS), ROW_CHUNKS)]
        return pltpu.make_async_copy(src, dst, sem)

    def start(t, _):
        for k in range(TOP_K):
            row_copy(t, k).start()
        return 0

    def wait(t, _):
        for k in range(TOP_K):
            row_copy(t, k).wait()
        return 0

    lax.fori_loop(0, rows, start, 0)
    lax.fori_loop(0, rows, wait, 0)

    wts = wts_ref[...]
    w_cols = [wts[:, k:k + 1] for k in range(TOP_K)]
    tot = jnp.zeros((rows, 1), F32)
    for j in range(ROW_CHUNKS):
        zj = DEEPNORM_ALPHA * hrow_ref[pl.ds(j, rows, stride=ROW_CHUNKS), :]
        for k in range(TOP_K):
            zj = zj + w_cols[k] * ybuf[k, pl.ds(j, rows, stride=ROW_CHUNKS), :]
        z_scr[:, j * LANES:(j + 1) * LANES] = zj
        tot = tot + jnp.sum(zj, axis=-1, keepdims=True)
    mu = tot / D_MODEL
    zc = z_scr[...] - mu
    var = jnp.mean(zc * zc, axis=-1, keepdims=True)
    h = zc * lax.rsqrt(var + LN_EPS) * g_ref[...] + b_ref[...]
    h_ref[...] = h
    hb_ref[...] = h.astype(BF16)
    for s in range(n_side):
        whi_ref, wlo_ref, sb_ref = side_refs[3 * s:3 * s + 3]
        outs[2 + s][...] = _dot3(h, whi_ref, wlo_ref) + sb_ref[...]


def _combine_ln(h_rows, y_rows, pos, wts, g, b, side=()):
    t = wts.shape[0]
    rows = COMBINE_ROWS
    const = lambda i, *_: (0, 0)
    side_args, side_specs = [], []
    for w_pad, bias in side:
        whi, wlo = _split_hi_lo(w_pad)
        side_args += [whi, wlo, bias.reshape(1, LANES)]
        side_specs += [pl.BlockSpec((D_MODEL, LANES), const), pl.BlockSpec((D_MODEL, LANES), const),
                       pl.BlockSpec((1, LANES), const)]
    n_side = len(side)
    outs = pl.pallas_call(
        functools.partial(_combine_ln_body, n_side=n_side),
        grid_spec=pltpu.PrefetchScalarGridSpec(
            num_scalar_prefetch=1,
            grid=(t // rows,),
            in_specs=[pl.BlockSpec((rows * ROW_CHUNKS, LANES), lambda i, *_: (i, 0)),
                      pl.BlockSpec((rows, LANES), lambda i, *_: (i, 0)),
                      pl.BlockSpec((1, D_MODEL), const), pl.BlockSpec((1, D_MODEL), const)]
            + side_specs + [pl.BlockSpec(memory_space=pl.ANY)],
            out_specs=[pl.BlockSpec((rows, D_MODEL), lambda i, *_: (i, 0)),
                       pl.BlockSpec((rows, D_MODEL), lambda i, *_: (i, 0))]
            + [pl.BlockSpec((rows, LANES), lambda i, *_: (i, 0))] * n_side,
            scratch_shapes=[pltpu.VMEM((TOP_K, rows * ROW_CHUNKS, LANES), F32),
                            pltpu.VMEM((rows, D_MODEL), F32),
                            pltpu.SemaphoreType.DMA(())]),
        out_shape=[jax.ShapeDtypeStruct((t, D_MODEL), F32), jax.ShapeDtypeStruct((t, D_MODEL), BF16)]
        + [jax.ShapeDtypeStruct((t, LANES), F32)] * n_side,
        compiler_params=_params(("arbitrary",)),
        name="moe_combine_ln",
    )(pos, h_rows, wts, g.reshape(1, -1), b.reshape(1, -1), *side_args, y_rows)
    return outs


def _moe_block(res, mix, ln_g, ln_b, router_w, router_b, w_gu, b_gu, w_down, b_down, fg, fb, side=()):
    h_rows, ids, wts, cnt = _ln_router(res, mix, ln_g, ln_b, router_w, router_b)
    pos_pad, te_pad, meta = _rank(ids, cnt)
    pos = pos_pad[:, :TOP_K].reshape(-1)
    base, count, pcount = meta[0, :N_EXPERTS], meta[1, :N_EXPERTS], meta[2, :N_EXPERTS]
    n_tiles = (meta[3, N_EXPERTS - 1:N_EXPERTS] // MOE_TM).astype(jnp.int32)
    n_slots = _moe_tiles(res.shape[0]) * MOE_TM
    pcount = pcount.at[N_EXPERTS - 1].set(n_slots - base[N_EXPERTS - 1])
    xs = _dispatch(h_rows, pos, base, count, pcount)
    y = _grouped_ffn(xs, te_pad[:, 0], n_tiles, w_gu.astype(BF16), b_gu, w_down.astype(BF16), b_down)
    return _combine_ln(h_rows, y, pos, wts, fg, fb, side)


def _compress_body(c_ref, pet_ref, peb_ref, w1t_ref, w1b_ref, b1_ref, w2_ref, b2_ref, cos_ref, sin_ref,
                   o_ref, *, use_rope):
    rows = c_ref.shape[0]
    c = c_ref[...].astype(F32)
    top = jnp.dot((c + pet_ref[...]).astype(BF16), w1t_ref[...], preferred_element_type=F32)
    bot = jnp.dot((c + peb_ref[...]).astype(BF16), w1b_ref[...], preferred_element_type=F32)
    pre = top + pltpu.roll(bot, rows - 1, axis=0) + b1_ref[...]
    out = jnp.dot(jax.nn.gelu(pre).astype(BF16), w2_ref[...], preferred_element_type=F32) + b2_ref[...]
    if use_rope:
        out = out * cos_ref[...] + pltpu.roll(out, HEAD_DIM // 2, axis=1) * sin_ref[...]
    o_ref[...] = out.astype(o_ref.dtype)


def _compress(chunks, pe, w1, b1, w2, b2, cos, sin, use_rope):
    r = chunks.shape[0]
    rows = 4 * N_CHUNK
    half = CMP_STRIDE * HEAD_DIM
    const = lambda i: (0, 0)
    reps = rows // N_CHUNK
    return pl.pallas_call(
        functools.partial(_compress_body, use_rope=use_rope),
        grid=(r // rows,),
        in_specs=[pl.BlockSpec((rows, half), lambda i: (i, 0)),
                  pl.BlockSpec((1, half), const), pl.BlockSpec((1, half), const),
                  pl.BlockSpec((half, CMP_HIDDEN), const), pl.BlockSpec((half, CMP_HIDDEN), const),
                  pl.BlockSpec((1, CMP_HIDDEN), const),
                  pl.BlockSpec((CMP_HIDDEN, HEAD_DIM), const), pl.BlockSpec((1, HEAD_DIM), const),
                  pl.BlockSpec((rows, HEAD_DIM), const), pl.BlockSpec((rows, HEAD_DIM), const)],
        out_specs=pl.BlockSpec((rows, HEAD_DIM), lambda i: (i, 0)),
        out_shape=jax.ShapeDtypeStruct((r, HEAD_DIM), BF16),
        compiler_params=_params(("arbitrary",)),
        name="nsa_compress",
    )(chunks, pe[:CMP_STRIDE].reshape(1, half), pe[CMP_STRIDE:].reshape(1, half),
      w1[:half].astype(BF16), w1[half:].astype(BF16), b1.reshape(1, -1),
      w2.astype(BF16), b2.reshape(1, -1), jnp.tile(cos, (reps, 1)), jnp.tile(sin, (reps, 1)))


def _cmp_attn_body(q_ref, kc_ref, vc_ref, ovt_ref, o_ref, selb_ref):
    s = q_ref.shape[0]
    kc = kc_ref[...]
    vc = vc_ref[...]
    pos = lax.broadcasted_iota(jnp.int32, (s, N_CHUNK), 0)
    cidx = lax.broadcasted_iota(jnp.int32, (s, N_CHUNK), 1)
    mask = (cidx * CMP_STRIDE + (CMP_BLOCK - 1) <= pos) & (cidx < N_CMP)
    psum = jnp.zeros((s, N_CHUNK), F32)
    for h in range(HEADS_PER_GROUP):
        qh = q_ref[:, h * HEAD_DIM:(h + 1) * HEAD_DIM]
        sc = lax.dot_general(qh, kc, (((1,), (1,)), ((), ())), preferred_element_type=F32)
        sc = jnp.where(mask, sc, NEG_BIG)
        m = jnp.max(sc, axis=-1, keepdims=True)
        e = jnp.where(mask, jnp.exp(sc - m), 0.0)
        l = jnp.sum(e, axis=-1, keepdims=True)
        p = e / jnp.where(l > 0.0, l, 1.0)
        psum = psum + p
        o_ref[:, h * HEAD_DIM:(h + 1) * HEAD_DIM] = jnp.dot(
            p.astype(BF16), vc, preferred_element_type=F32).astype(o_ref.dtype)
    ph = psum.astype(BF16)
    pl_ = (psum - ph.astype(F32)).astype(BF16)
    ovt = ovt_ref[...]
    dn = (((1,), (1,)), ((), ()))
    imp = (lax.dot_general(ovt, ph, dn, preferred_element_type=F32)
           + lax.dot_general(ovt, pl_, dn, preferred_element_type=F32))[:N_SLC]
    blk = lax.broadcasted_iota(jnp.int32, (N_SLC, s), 0)
    qpos = lax.broadcasted_iota(jnp.int32, (N_SLC, s), 1)
    cur = qpos // SLC_BLOCK
    causal = blk * SLC_BLOCK <= qpos
    forced = (blk == 0) | (blk == cur) | (blk == cur - 1)
    val = jnp.where(causal, imp + jnp.where(forced, FORCE_BONUS, 0.0), -jnp.inf)
    rank = jnp.zeros((N_SLC, s), jnp.int32)
    for m_ in range(N_SLC):
        row = val[m_:m_ + 1, :]
        tie = jnp.where(blk > m_, 1, 0)
        rank = rank + jnp.where(row > val, 1, jnp.where(row == val, tie, 0))
    sel = causal & (rank < SLC_TOP_N)
    bias = jnp.where(sel, 0.0, NEG_BIG)
    bias = jnp.concatenate([bias, jnp.zeros((LANES - N_SLC, s), F32)], axis=0)
    selb_ref[...] = bias.T.astype(selb_ref.dtype)


def _cmp_attn(q, k_cmp, v_cmp, batch):
    t = batch * SEQ
    gw = HEADS_PER_GROUP * HEAD_DIM
    c0 = jnp.arange(N_CHUNK)[None, :] * CMP_STRIDE
    s0 = jnp.arange(LANES)[:, None] * SLC_BLOCK
    ovt = ((c0 < s0 + SLC_BLOCK) & (c0 + CMP_BLOCK > s0) & (jnp.arange(LANES)[:, None] < N_SLC)
           & (jnp.arange(N_CHUNK)[None, :] < N_CMP)).astype(BF16)
    return pl.pallas_call(
        _cmp_attn_body,
        grid=(batch, N_KV_GROUPS),
        in_specs=[pl.BlockSpec((SEQ, gw), lambda b, g: (b, g)),
                  pl.BlockSpec((N_CHUNK, HEAD_DIM), lambda b, g: (b * N_KV_GROUPS + g, 0)),
                  pl.BlockSpec((N_CHUNK, HEAD_DIM), lambda b, g: (b * N_KV_GROUPS + g, 0)),
                  pl.BlockSpec((LANES, N_CHUNK), lambda b, g: (0, 0))],
        out_specs=[pl.BlockSpec((SEQ, gw), lambda b, g: (b, g)),
                   pl.BlockSpec((SEQ, LANES), lambda b, g: (b * N_KV_GROUPS + g, 0))],
        out_shape=[jax.ShapeDtypeStruct((t, Q_WIDTH), BF16),
                   jax.ShapeDtypeStruct((batch * N_KV_GROUPS * SEQ, LANES), BF16)],
        compiler_params=_params(("arbitrary", "arbitrary")),
        name="nsa_cmp_attn",
    )(q, k_cmp, v_cmp, ovt)


def _softmax_pv(pieces):
    m = functools.reduce(jnp.maximum, [jnp.max(sc, axis=-1, keepdims=True) for sc, _ in pieces])
    l = 0.0
    o = 0.0
    for sc, v in pieces:
        e = jnp.exp(sc - m)
        l = l + jnp.sum(e, axis=-1, keepdims=True)
        o = o + jnp.dot(e.astype(BF16), v, preferred_element_type=F32)
    return o / l


def _attn_body(q_ref, selb_ref, ks_ref, vs_ref, kw_ref, vw_ref, oc_ref, gl_ref, o_ref, kaug_scr):
    s = q_ref.shape[0]
    tq = ATT_TQ
    hh = pl.program_id(1) * HEADS_PER_GROUP + pl.program_id(2)
    dn = (((1,), (1,)), ((), ()))
    key = lax.broadcasted_iota(jnp.int32, (s, LANES), 0)
    lane = lax.broadcasted_iota(jnp.int32, (s, LANES), 1)
    kaug_scr[:, :HEAD_DIM] = ks_ref[...]
    kaug_scr[:, HEAD_DIM:] = jnp.where(key // SLC_BLOCK == lane, 1.0, 0.0).astype(BF16)
    r_i = lax.broadcasted_iota(jnp.int32, (tq, tq), 0)
    c_i = lax.broadcasted_iota(jnp.int32, (tq, tq), 1)
    causal_bias = jnp.where(c_i <= r_i, 0.0, NEG_BIG)
    band_bias = jnp.where(c_i > r_i, 0.0, NEG_BIG)
    glane = lax.broadcasted_iota(jnp.int32, (tq, LANES), 1)
    for i in range(s // tq):
        q0 = i * tq
        q = q_ref[q0:q0 + tq, :]
        qa = jnp.concatenate([q, selb_ref[q0:q0 + tq, :]], axis=1)
        pieces = []
        if i > 0:
            pieces.append((lax.dot_general(qa, kaug_scr[0:q0, :], dn, preferred_element_type=F32),
                           vs_ref[0:q0, :]))
        pieces.append((lax.dot_general(qa, kaug_scr[q0:q0 + tq, :], dn, preferred_element_type=F32) + causal_bias,
                       vs_ref[q0:q0 + tq, :]))
        o_slc = _softmax_pv(pieces)
        pieces = []
        lo = q0 - WINDOW
        if lo >= 0:
            pieces.append((lax.dot_general(q, kw_ref[lo:lo + tq, :], dn, preferred_element_type=F32) + band_bias,
                           vw_ref[lo:lo + tq, :]))
        mid = max(lo + tq, 0)
        if q0 > mid:
            pieces.append((lax.dot_general(q, kw_ref[mid:q0, :], dn, preferred_element_type=F32),
                           vw_ref[mid:q0, :]))
        pieces.append((lax.dot_general(q, kw_ref[q0:q0 + tq, :], dn, preferred_element_type=F32) + causal_bias,
                       vw_ref[q0:q0 + tq, :]))
        o_win = _softmax_pv(pieces)
        gates = jax.nn.sigmoid(gl_ref[q0:q0 + tq, :])
        gsel = [jnp.sum(jnp.where(glane == hh * N_BRANCHES + br, gates, 0.0), axis=-1, keepdims=True)
                for br in range(N_BRANCHES)]
        o = gsel[0] * oc_ref[q0:q0 + tq, :].astype(F32) + gsel[1] * o_slc + gsel[2] * o_win
        o_ref[q0:q0 + tq, :] = o.astype(o_ref.dtype)


def _attention(q, selb, kv, o_cmp, gate_logits, batch):
    t = batch * SEQ
    g_ = N_KV_GROUPS
    head = lambda b, g, h: (b, g * HEADS_PER_GROUP + h)
    slot = lambda n: (lambda b, g, h: (b, n * g_ + g))
    return pl.pallas_call(
        _attn_body,
        grid=(batch, N_KV_GROUPS, HEADS_PER_GROUP),
        in_specs=[pl.BlockSpec((SEQ, HEAD_DIM), head),
                  pl.BlockSpec((SEQ, LANES), lambda b, g, h: (b * g_ + g, 0)),
                  pl.BlockSpec((SEQ, HEAD_DIM), slot(2)), pl.BlockSpec((SEQ, HEAD_DIM), slot(3)),
                  pl.BlockSpec((SEQ, HEAD_DIM), slot(4)), pl.BlockSpec((SEQ, HEAD_DIM), slot(5)),
                  pl.BlockSpec((SEQ, HEAD_DIM), head),
                  pl.BlockSpec((SEQ, LANES), lambda b, g, h: (b, 0))],
        out_specs=pl.BlockSpec((SEQ, HEAD_DIM), head),
        out_shape=jax.ShapeDtypeStruct((t, Q_WIDTH), BF16),
        scratch_shapes=[pltpu.VMEM((SEQ, 2 * HEAD_DIM), BF16)],
        compiler_params=_params(("arbitrary", "arbitrary", "arbitrary")),
        name="nsa_attn",
    )(q, selb, kv, kv, kv, kv, o_cmp, gate_logits)


def _nsa_mixer(hb, gate_logits, kv_w, w_in, w_out, cmp_k, cmp_v, batch):
    cos, sin = _rope_tables(jnp.arange(SEQ))
    kv = _matmul_rope(hb, kv_w.astype(BF16), cos, sin, BF16, "nsa_kv_proj", scale=1.0,
                      rope_cols=(2, 4), bm=MM_BM, bn=KV_SLOT_W)
    q = _matmul_rope(hb, w_in[:, :Q_WIDTH].astype(BF16), cos, sin, BF16, "nsa_q_proj",
                     scale=HEAD_DIM ** -0.5, rope_cols=None, bm=MM_BM, bn=MM_BN)

    def chunks(slot):
        c = kv[:, slot * KV_SLOT_W:(slot + 1) * KV_SLOT_W]
        c = c.reshape(batch, N_CHUNK, CMP_STRIDE, N_KV_GROUPS, HEAD_DIM).transpose(0, 3, 1, 2, 4)
        return c.reshape(batch * N_KV_GROUPS * N_CHUNK, CMP_STRIDE * HEAD_DIM)

    ccos, csin = _rope_tables(jnp.arange(N_CHUNK) * CMP_STRIDE + CMP_BLOCK - 1)
    k_cmp = _compress(chunks(0), *cmp_k, ccos, csin, True)
    v_cmp = _compress(chunks(1), *cmp_v, ccos, csin, False)
    o_cmp, selb = _cmp_attn(q, k_cmp, v_cmp, batch)
    o = _attention(q, selb, kv, o_cmp, gate_logits, batch)
    return _matmul(o, w_out.astype(BF16), F32, "nsa_out_proj")


def kernel(x, ln_mix_g, ln_mix_b, ln_ffn_g, ln_ffn_b, a_w_in, a_conv_w, a_conv_b, a_gate_a_w, a_gate_a_b,
           a_gate_x_w, a_gate_x_b, a_lambda, a_w_out, b_w_in, b_w_out, kv_w, cmp_k_pe, cmp_k_w1, cmp_k_b1,
           cmp_k_w2, cmp_k_b2, cmp_v_pe, cmp_v_w1, cmp_v_b1, cmp_v_w2, cmp_v_b2, moe_router_w, moe_router_b,
           moe_w_gu, moe_b_gu, moe_w_down, moe_b_down):
    batch = x.shape[0]
    t = batch * SEQ
    xf = x.reshape(t, D_MODEL)

    def moe(layer, res, mix, side=()):
        return _moe_block(res, mix, ln_mix_g[layer], ln_mix_b[layer], moe_router_w[layer], moe_router_b[layer],
                          moe_w_gu[layer], moe_b_gu[layer], moe_w_down[layer], moe_b_down[layer],
                          ln_ffn_g[layer], ln_ffn_b[layer], side)

    proj = _matmul(xf.astype(BF16), a_w_in[0].astype(BF16), BF16, "rg_in_proj")
    rec = _rglru(proj, a_conv_w[0], a_conv_b[0], a_gate_a_w[0].astype(BF16), a_gate_a_b[0],
                 a_gate_x_w[0].astype(BF16), a_gate_x_b[0], a_lambda[0], batch)
    mix = _matmul(rec, a_w_out[0].astype(BF16), F32, "rg_out_proj")
    n_gate = N_BRANCHES * N_HEADS
    gate_w = jnp.pad(b_w_in[0][:, Q_WIDTH:], ((0, 0), (0, LANES - n_gate)))
    h, hb, gate_logits = moe(0, xf, mix, side=((gate_w, jnp.zeros((LANES,), F32)),))

    mix = _nsa_mixer(hb, gate_logits, kv_w, b_w_in[0], b_w_out[0],
                     (cmp_k_pe, cmp_k_w1, cmp_k_b1, cmp_k_w2, cmp_k_b2),
                     (cmp_v_pe, cmp_v_w1, cmp_v_b1, cmp_v_w2, cmp_v_b2), batch)
    h, _ = moe(1, h, mix)
    return h.reshape(batch, SEQ, D_MODEL)
```

```python
import functools

import jax
import jax.numpy as jnp
from jax import lax
from jax.experimental import pallas as pl
from jax.experimental.pallas import tpu as pltpu

F32 = jnp.float32
BF16 = jnp.bfloat16

D_MODEL = 4096
SEQ = 2048
DEPTH = 2
DEEPNORM_ALPHA = (2 * DEPTH) ** 0.25
LN_EPS = 1e-5
RNN_WIDTH = D_MODEL
RG_BLOCKS = 16
RG_BLOCK_W = RNN_WIDTH // RG_BLOCKS
CONV_WIDTH = 4
RG_C = 8.0
N_HEADS = 32
HEAD_DIM = 128
N_KV_GROUPS = 4
HEADS_PER_GROUP = N_HEADS // N_KV_GROUPS
N_BRANCHES = 3
Q_WIDTH = N_HEADS * HEAD_DIM
KV_SLOT_W = N_KV_GROUPS * HEAD_DIM
KV_WIDTH = 2 * N_BRANCHES * KV_SLOT_W
CMP_BLOCK = 32
CMP_STRIDE = 16
CMP_HIDDEN = 512
N_CMP = (SEQ - CMP_BLOCK) // CMP_STRIDE + 1
N_CHUNK = SEQ // CMP_STRIDE
SLC_BLOCK = 64
N_SLC = SEQ // SLC_BLOCK
SLC_TOP_N = min(16, N_SLC)
WINDOW = 512
ROPE_THETA = 10000.0
NEG_BIG = -1e30
FORCE_BONUS = 1e4
N_EXPERTS = 32
TOP_K = 4
EXPERT_FF = D_MODEL // 8
SWIGLU_ALPHA = 1.702
SWIGLU_LIMIT = 7.0

LANES = 128
SUBLANES = 8
HALF_D = D_MODEL // 2
VMEM_LIMIT = 56 * 1024 * 1024

MM_BM = 1024
MM_BN = 1024
LN_ROWS = 256
COMBINE_ROWS = 128
RANK_ROWS = 512
MOE_TM = 512
FFN_OUT_COLS = 512
CAST_ROWS = 2048
ATT_TQ = 256


def _params(sem, vmem=VMEM_LIMIT):
    return pltpu.CompilerParams(dimension_semantics=sem, vmem_limit_bytes=vmem)


def _mm_body(x_ref, w_ref, o_ref):
    o_ref[...] = jnp.dot(x_ref[...], w_ref[...], preferred_element_type=F32).astype(o_ref.dtype)


def _matmul(x, w, out_dtype, name, bm=MM_BM, bn=MM_BN):
    m, k = x.shape
    n = w.shape[1]
    return pl.pallas_call(
        _mm_body,
        grid=(n // bn, m // bm),
        in_specs=[pl.BlockSpec((bm, k), lambda j, i: (i, 0)),
                  pl.BlockSpec((k, bn), lambda j, i: (0, j))],
        out_specs=pl.BlockSpec((bm, bn), lambda j, i: (i, j)),
        out_shape=jax.ShapeDtypeStruct((m, n), out_dtype),
        compiler_params=_params(("arbitrary", "arbitrary")),
        name=name,
    )(x, w)


def _mm_rope_body(x_ref, w_ref, cos_ref, sin_ref, o_ref, *, scale, rope_cols):
    acc = jnp.dot(x_ref[...], w_ref[...], preferred_element_type=F32)
    cos = cos_ref[...]
    sin = sin_ref[...]
    if rope_cols is not None:
        j = pl.program_id(0)
        flag = functools.reduce(jnp.logical_or, [j == c for c in rope_cols])
        cos = jnp.where(flag, cos, 1.0)
        sin = jnp.where(flag, sin, 0.0)
    for h in range(acc.shape[1] // HEAD_DIM):
        a = acc[:, h * HEAD_DIM:(h + 1) * HEAD_DIM]
        r = a * cos + pltpu.roll(a, HEAD_DIM // 2, axis=1) * sin
        o_ref[:, h * HEAD_DIM:(h + 1) * HEAD_DIM] = (r * scale).astype(o_ref.dtype)


def _matmul_rope(x, w, cos, sin, out_dtype, name, *, scale, rope_cols, bm, bn):
    m, k = x.shape
    n = w.shape[1]
    s_blocks = SEQ // bm
    return pl.pallas_call(
        functools.partial(_mm_rope_body, scale=scale, rope_cols=rope_cols),
        grid=(n // bn, m // bm),
        in_specs=[pl.BlockSpec((bm, k), lambda j, i: (i, 0)),
                  pl.BlockSpec((k, bn), lambda j, i: (0, j)),
                  pl.BlockSpec((bm, HEAD_DIM), lambda j, i: (i % s_blocks, 0)),
                  pl.BlockSpec((bm, HEAD_DIM), lambda j, i: (i % s_blocks, 0))],
        out_specs=pl.BlockSpec((bm, bn), lambda j, i: (i, j)),
        out_shape=jax.ShapeDtypeStruct((m, n), out_dtype),
        compiler_params=_params(("arbitrary", "arbitrary")),
        name=name,
    )(x, w, cos, sin)


def _rope_tables(pos):
    half = HEAD_DIM // 2
    inv = ROPE_THETA ** (-jnp.arange(half, dtype=F32) / half)
    ang = pos.astype(F32)[:, None] * inv[None, :]
    cos = jnp.cos(ang)
    sin = jnp.sin(ang)
    return jnp.concatenate([cos, cos], axis=1), jnp.concatenate([-sin, sin], axis=1)


def _rglru_body(y_ref, u_ref, cw_ref, cb_ref, wa_ref, ba_ref, wx_ref, bx_ref, lam_ref, o_ref,
                a_scr, b_scr):
    s = u_ref.shape[0]
    u_in = u_ref[...].astype(F32)
    row = lax.broadcasted_iota(jnp.int32, u_in.shape, 0)
    u = u_in * cw_ref[CONV_WIDTH - 1:CONV_WIDTH, :] + cb_ref[...]
    for d in range(1, CONV_WIDTH):
        shifted = jnp.where(row >= d, pltpu.roll(u_in, d, axis=0), 0.0)
        u = u + shifted * cw_ref[CONV_WIDTH - 1 - d:CONV_WIDTH - d, :]
    ub = u.astype(BF16)
    gate_x = jax.nn.sigmoid(jnp.dot(ub, wx_ref[0], preferred_element_type=F32) + bx_ref[0])
    gate_a = jax.nn.sigmoid(jnp.dot(ub, wa_ref[0], preferred_element_type=F32) + ba_ref[0])
    z = -lam_ref[...]
    softplus = jnp.maximum(z, 0.0) + jnp.log1p(jnp.exp(-jnp.abs(z)))
    log_a = -RG_C * gate_a * softplus
    a = jnp.exp(log_a)
    th = jnp.tanh(log_a)
    b = jnp.sqrt(-2.0 * th / (1.0 - th)) * (gate_x * u)
    sub = row % SUBLANES
    for d in (1, 2, 4):
        valid = sub >= d
        a_prev = pltpu.roll(a, d, axis=0)
        b_prev = pltpu.roll(b, d, axis=0)
        b = jnp.where(valid, a * b_prev + b, b)
        a = jnp.where(valid, a * a_prev, a)
    a_scr[...] = a
    b_scr[...] = b

    def step(g, carry):
        r0 = pl.multiple_of(g * SUBLANES, SUBLANES)
        hs = a_scr[pl.ds(r0, SUBLANES), :] * carry + b_scr[pl.ds(r0, SUBLANES), :]
        b_scr[pl.ds(r0, SUBLANES), :] = hs
        return jnp.broadcast_to(hs[SUBLANES - 1:SUBLANES, :], hs.shape)

    lax.fori_loop(0, s // SUBLANES, step, jnp.zeros((SUBLANES, u_in.shape[1]), F32))
    y = jax.nn.gelu(y_ref[...].astype(F32))
    o_ref[...] = (b_scr[...] * y).astype(o_ref.dtype)


def _rglru(proj, conv_w, conv_b, gate_a_w, gate_a_b, gate_x_w, gate_x_b, lam, batch):
    w = RG_BLOCK_W
    t = batch * SEQ
    vec = lambda b, n: (0, n)
    blk = lambda b, n: (n, 0, 0)
    return pl.pallas_call(
        _rglru_body,
        grid=(batch, RG_BLOCKS),
        in_specs=[pl.BlockSpec((SEQ, w), lambda b, n: (b, n)),
                  pl.BlockSpec((SEQ, w), lambda b, n: (b, RG_BLOCKS + n)),
                  pl.BlockSpec((CONV_WIDTH, w), vec),
                  pl.BlockSpec((1, w), vec),
                  pl.BlockSpec((1, w, w), blk), pl.BlockSpec((1, 1, w), blk),
                  pl.BlockSpec((1, w, w), blk), pl.BlockSpec((1, 1, w), blk),
                  pl.BlockSpec((1, w), vec)],
        out_specs=pl.BlockSpec((SEQ, w), lambda b, n: (b, n)),
        out_shape=jax.ShapeDtypeStruct((t, RNN_WIDTH), BF16),
        scratch_shapes=[pltpu.VMEM((SEQ, w), F32), pltpu.VMEM((SEQ, w), F32)],
        compiler_params=_params(("arbitrary", "arbitrary")),
        name="rglru",
    )(proj, proj, conv_w, conv_b.reshape(1, -1),
      gate_a_w, gate_a_b.reshape(RG_BLOCKS, 1, w), gate_x_w, gate_x_b.reshape(RG_BLOCKS, 1, w),
      lam.reshape(1, -1))


def _split_hi_lo(w):
    hi = w.astype(BF16)
    lo = (w - hi.astype(F32)).astype(BF16)
    return hi, lo


def _dot3(x, whi_ref, wlo_ref):
    xh = x.astype(BF16)
    xl = (x - xh.astype(F32)).astype(BF16)
    whi = whi_ref[...]
    return (jnp.dot(xh, whi, preferred_element_type=F32)
            + jnp.dot(xl, whi, preferred_element_type=F32)
            + jnp.dot(xh, wlo_ref[...], preferred_element_type=F32))


def _layer_norm(z, g, b):
    mu = jnp.mean(z, axis=-1, keepdims=True)
    zc = z - mu
    var = jnp.mean(zc * zc, axis=-1, keepdims=True)
    return zc * lax.rsqrt(var + LN_EPS) * g + b


def _lane_pack(cols, lane):
    out = jnp.broadcast_to(cols[-1], lane.shape)
    for k in range(len(cols) - 2, -1, -1):
        out = jnp.where(lane == k, cols[k], out)
    return out


def _pack_pair(lo, hi):
    lo_bits = pltpu.bitcast(lo.astype(BF16).astype(F32), jnp.uint32) >> 16
    hi_bits = pltpu.bitcast(hi.astype(BF16).astype(F32), jnp.uint32) & jnp.uint32(0xFFFF0000)
    return lo_bits | hi_bits


def _unpack_pair(words):
    lo = pltpu.bitcast(words << 16, F32)
    hi = pltpu.bitcast(words & jnp.uint32(0xFFFF0000), F32)
    return lo, hi


def _ln_router_body(res_ref, mix_ref, g_ref, b_ref, whi_ref, wlo_ref, rb_ref,
                    h_ref, hp_ref, ids_ref, wts_ref, cnt_ref):
    z = DEEPNORM_ALPHA * res_ref[...] + mix_ref[...].astype(F32)
    h = _layer_norm(z, g_ref[...], b_ref[...])
    h_ref[...] = h
    hp_ref[...] = _pack_pair(h[:, :HALF_D], h[:, HALF_D:])
    logits = _dot3(h, whi_ref, wlo_ref) + rb_ref[...]
    lane = lax.broadcasted_iota(jnp.int32, logits.shape, 1)
    cur = jnp.where(lane < N_EXPERTS, logits, -jnp.inf)
    vals, idxs = [], []
    for _ in range(TOP_K):
        m = jnp.max(cur, axis=-1, keepdims=True)
        idx = jnp.min(jnp.where(cur == m, lane, LANES), axis=-1, keepdims=True)
        vals.append(m)
        idxs.append(idx)
        cur = jnp.where(lane == idx, -jnp.inf, cur)
    es = [jnp.exp(v - vals[0]) for v in vals]
    tot = functools.reduce(lambda p, q: p + q, es)
    ids_ref[...] = _lane_pack(idxs, lane)
    wts_ref[...] = _lane_pack([e / tot for e in es], lane)
    hot = functools.reduce(lambda p, q: p + q, [(lane == i).astype(F32) for i in idxs])

    @pl.when(pl.program_id(0) == 0)
    def _():
        cnt_ref[...] = jnp.zeros_like(cnt_ref)

    cnt_ref[...] += jnp.broadcast_to(jnp.sum(hot, axis=0, keepdims=True), cnt_ref.shape)


def _ln_router(res, mix, g, b, router_w, router_b):
    t = res.shape[0]
    rows = LN_ROWS
    w_pad = jnp.pad(router_w, ((0, 0), (0, LANES - N_EXPERTS)))
    whi, wlo = _split_hi_lo(w_pad)
    rb = jnp.pad(router_b, (0, LANES - N_EXPERTS)).reshape(1, LANES)
    const = lambda i: (0, 0)
    return pl.pallas_call(
        _ln_router_body,
        grid=(t // rows,),
        in_specs=[pl.BlockSpec((rows, D_MODEL), lambda i: (i, 0)),
                  pl.BlockSpec((rows, D_MODEL), lambda i: (i, 0)),
                  pl.BlockSpec((1, D_MODEL), const), pl.BlockSpec((1, D_MODEL), const),
                  pl.BlockSpec((D_MODEL, LANES), const), pl.BlockSpec((D_MODEL, LANES), const),
                  pl.BlockSpec((1, LANES), const)],
        out_specs=[pl.BlockSpec((rows, D_MODEL), lambda i: (i, 0)),
                   pl.BlockSpec((rows, HALF_D), lambda i: (i, 0)),
                   pl.BlockSpec((rows, LANES), lambda i: (i, 0)),
                   pl.BlockSpec((rows, LANES), lambda i: (i, 0)),
                   pl.BlockSpec((SUBLANES, LANES), const)],
        out_shape=[jax.ShapeDtypeStruct((t, D_MODEL), F32),
                   jax.ShapeDtypeStruct((t, HALF_D), jnp.uint32),
                   jax.ShapeDtypeStruct((t, LANES), jnp.int32),
                   jax.ShapeDtypeStruct((t, LANES), F32),
                   jax.ShapeDtypeStruct((SUBLANES, LANES), F32)],
        compiler_params=_params(("arbitrary",)),
        name="ln_router",
    )(res, mix, g.reshape(1, -1), b.reshape(1, -1), whi, wlo, rb)


def _moe_tiles(t):
    return (t * TOP_K) // MOE_TM + N_EXPERTS


def _rank_body(ids_ref, cnt_ref, pos_ref, te_ref, meta_ref, carry_scr, base_scr):
    rows = ids_ref.shape[0]
    lane8 = lax.broadcasted_iota(jnp.int32, (SUBLANES, LANES), 1)

    @pl.when(pl.program_id(0) == 0)
    def _():
        cnt = cnt_ref[...]
        padded = jnp.floor((cnt + (MOE_TM - 1)) / MOE_TM) * MOE_TM
        padded = jnp.where(lane8 < N_EXPERTS, padded, 0.0)
        ends = padded
        d = 1
        while d < N_EXPERTS:
            ends = ends + jnp.where(lane8 >= d, pltpu.roll(ends, d, axis=1), 0.0)
            d *= 2
        base = ends - padded
        base_scr[...] = base
        carry_scr[...] = jnp.zeros_like(carry_scr)
        sub8 = lax.broadcasted_iota(jnp.int32, (SUBLANES, LANES), 0)
        meta = jnp.where(sub8 == 0, base, jnp.where(sub8 == 1, cnt, jnp.where(sub8 == 2, padded, ends)))
        meta_ref[...] = meta.astype(jnp.int32)
        tstart = lax.broadcasted_iota(jnp.int32, te_ref.shape, 0).astype(F32) * MOE_TM
        lane_t = lax.broadcasted_iota(jnp.int32, te_ref.shape, 1)
        done = jnp.where(lane_t < N_EXPERTS, (ends[0:1, :] <= tstart).astype(F32), 0.0)
        te = jnp.minimum(jnp.sum(done, axis=-1, keepdims=True), N_EXPERTS - 1.0)
        te_ref[...] = jnp.broadcast_to(te, te_ref.shape).astype(jnp.int32)

    ids = ids_ref[...]
    lane = lax.broadcasted_iota(jnp.int32, ids.shape, 1)
    hits = [ids[:, k:k + 1] == lane for k in range(TOP_K)]
    hot = functools.reduce(lambda p, q: p + q, [h.astype(F32) for h in hits])
    r_i = lax.broadcasted_iota(jnp.int32, (rows, rows), 0)
    c_i = lax.broadcasted_iota(jnp.int32, (rows, rows), 1)
    lower = jnp.where(r_i > c_i, 1.0, 0.0).astype(BF16)
    before = jnp.dot(lower, hot.astype(BF16), preferred_element_type=F32)
    slot = before + carry_scr[0:1, :] + base_scr[0:1, :]
    cols = [jnp.sum(jnp.where(h, slot, 0.0), axis=-1, keepdims=True) for h in hits]
    pos_ref[...] = _lane_pack(cols, lane).astype(jnp.int32)
    carry_scr[...] += jnp.broadcast_to(jnp.sum(hot, axis=0, keepdims=True), carry_scr.shape)


def _rank(ids, cnt):
    t = ids.shape[0]
    rows = RANK_ROWS
    nt_pad = -(-_moe_tiles(t) // SUBLANES) * SUBLANES
    const = lambda i: (0, 0)
    return pl.pallas_call(
        _rank_body,
        grid=(t // rows,),
        in_specs=[pl.BlockSpec((rows, LANES), lambda i: (i, 0)),
                  pl.BlockSpec((SUBLANES, LANES), const)],
        out_specs=[pl.BlockSpec((rows, LANES), lambda i: (i, 0)),
                   pl.BlockSpec((nt_pad, LANES), const),
                   pl.BlockSpec((SUBLANES, LANES), const)],
        out_shape=[jax.ShapeDtypeStruct((t, LANES), jnp.int32),
                   jax.ShapeDtypeStruct((nt_pad, LANES), jnp.int32),
                   jax.ShapeDtypeStruct((SUBLANES, LANES), jnp.int32)],
        scratch_shapes=[pltpu.VMEM((SUBLANES, LANES), F32), pltpu.VMEM((SUBLANES, LANES), F32)],
        compiler_params=_params(("arbitrary",)),
        name="moe_rank",
    )(ids, cnt)


def _dispatch_body(pos_ref, base_ref, cnt_ref, pcnt_ref, h_ref, xs_ref, zero_scr, sem):
    i = pl.program_id(0)
    rows = h_ref.shape[0]

    def row_copy(src_row_ref, slot):
        return pltpu.make_async_copy(src_row_ref, xs_ref.at[pl.ds(slot, 1), :], sem)

    @pl.when(i == 0)
    def _():
        zero_scr[...] = jnp.zeros_like(zero_scr)
        zero_row = zero_scr.at[pl.ds(0, 1), :]

        def per_expert(e, _):
            def start(r, _):
                row_copy(zero_row, base_ref[e] + r).start()
                return 0

            def wait(r, _):
                row_copy(zero_row, base_ref[e] + r).wait()
                return 0

            lax.fori_loop(cnt_ref[e], pcnt_ref[e], start, 0)
            lax.fori_loop(cnt_ref[e], pcnt_ref[e], wait, 0)
            return 0

        lax.fori_loop(0, N_EXPERTS, per_expert, 0)

    def src(t):
        return h_ref.at[pl.ds(t, 1), :]

    def start(t, _):
        for k in range(TOP_K):
            row_copy(src(t), pos_ref[(i * rows + t) * TOP_K + k]).start()
        return 0

    def wait(t, _):
        for k in range(TOP_K):
            row_copy(src(t), pos_ref[(i * rows + t) * TOP_K + k]).wait()
        return 0

    lax.fori_loop(0, rows, start, 0)
    lax.fori_loop(0, rows, wait, 0)


def _dispatch(h_packed, pos, base, cnt, pcnt):
    t = h_packed.shape[0]
    rows = LN_ROWS
    p_pad = _moe_tiles(t) * MOE_TM
    return pl.pallas_call(
        _dispatch_body,
        grid_spec=pltpu.PrefetchScalarGridSpec(
            num_scalar_prefetch=4,
            grid=(t // rows,),
            in_specs=[pl.BlockSpec((rows, HALF_D), lambda i, *_: (i, 0))],
            out_specs=pl.BlockSpec(memory_space=pl.ANY),
            scratch_shapes=[pltpu.VMEM((SUBLANES, HALF_D), jnp.uint32), pltpu.SemaphoreType.DMA(())]),
        out_shape=jax.ShapeDtypeStruct((p_pad, HALF_D), jnp.uint32),
        compiler_params=_params(("arbitrary",)),
        name="moe_dispatch",
    )(pos, base, cnt, pcnt, h_packed)


def _ffn_body(te_ref, nt_ref, xs_ref, wgu_ref, bgu_ref, wd_ref, bd_ref, y_ref):
    @pl.when(pl.program_id(0) < nt_ref[0])
    def _():
        lo, hi = _unpack_pair(xs_ref[...])
        x = jnp.concatenate([lo.astype(BF16), hi.astype(BF16)], axis=1)
        gu = jnp.dot(x, wgu_ref[0], preferred_element_type=F32) + bgu_ref[0]
        gate = jnp.minimum(gu[:, :EXPERT_FF], SWIGLU_LIMIT)
        up = jnp.clip(gu[:, EXPERT_FF:], -SWIGLU_LIMIT, SWIGLU_LIMIT)
        act = ((up + 1.0) * gate * jax.nn.sigmoid(SWIGLU_ALPHA * gate)).astype(BF16)
        for c in range(0, HALF_D, FFN_OUT_COLS):
            d = c + HALF_D
            y_lo = jnp.dot(act, wd_ref[0, :, c:c + FFN_OUT_COLS], preferred_element_type=F32)
            y_hi = jnp.dot(act, wd_ref[0, :, d:d + FFN_OUT_COLS], preferred_element_type=F32)
            y_ref[:, c:c + FFN_OUT_COLS] = _pack_pair(y_lo + bd_ref[0, :, c:c + FFN_OUT_COLS],
                                                       y_hi + bd_ref[0, :, d:d + FFN_OUT_COLS])

    @pl.when(pl.program_id(0) >= nt_ref[0])
    def _():
        y_ref[...] = jnp.zeros_like(y_ref)


def _grouped_ffn(xs, tile_expert, n_tiles, w_gu, b_gu, w_down, b_down):
    p_pad = xs.shape[0]
    tm = MOE_TM
    row_blk = lambda i, te, nt: (jnp.minimum(i, nt[0] - 1), 0)
    exp_blk = lambda i, te, nt: (te[i], 0, 0)
    return pl.pallas_call(
        _ffn_body,
        grid_spec=pltpu.PrefetchScalarGridSpec(
            num_scalar_prefetch=2,
            grid=(p_pad // tm,),
            in_specs=[pl.BlockSpec((tm, HALF_D), row_blk),
                      pl.BlockSpec((1, D_MODEL, 2 * EXPERT_FF), exp_blk),
                      pl.BlockSpec((1, 1, 2 * EXPERT_FF), exp_blk),
                      pl.BlockSpec((1, EXPERT_FF, D_MODEL), exp_blk),
                      pl.BlockSpec((1, 1, D_MODEL), exp_blk)],
            out_specs=pl.BlockSpec((tm, HALF_D), lambda i, te, nt: (i, 0))),
        out_shape=jax.ShapeDtypeStruct(xs.shape, jnp.uint32),
        compiler_params=_params(("arbitrary",)),
        name="moe_ffn",
    )(tile_expert, n_tiles, xs, w_gu, b_gu.reshape(N_EXPERTS, 1, -1), w_down,
      b_down.reshape(N_EXPERTS, 1, -1))


def _combine_ln_body(pos_ref, res_ref, wts_ref, g_ref, b_ref, *rest, n_side, write_bf16):
    side_refs = rest[:3 * n_side]
    y_hbm = rest[3 * n_side]
    outs = rest[3 * n_side + 1:-2]
    ybuf, sem = rest[-2:]
    i = pl.program_id(0)
    n_steps = pl.num_programs(0)
    rows = wts_ref.shape[0]
    cur = i % 2

    def row_copy(step, buf, t, k):
        slot = pos_ref[(step * rows + t) * TOP_K + k]
        return pltpu.make_async_copy(y_hbm.at[pl.ds(slot, 1), :], ybuf.at[buf, k, pl.ds(t, 1), :], sem.at[buf])

    def issue(step, buf):
        def body(t, _):
            for k in range(TOP_K):
                row_copy(step, buf, t, k).start()
            return 0

        lax.fori_loop(0, rows, body, 0)

    @pl.when(i == 0)
    def _():
        issue(0, 0)

    @pl.when(i + 1 < n_steps)
    def _():
        issue(i + 1, 1 - cur)

    def wait(t, _):
        for k in range(TOP_K):
            row_copy(i, cur, t, k).wait()
        return 0

    lax.fori_loop(0, rows, wait, 0)

    wts = wts_ref[...]
    res = res_ref[...]
    z_lo = DEEPNORM_ALPHA * res[:, :HALF_D]
    z_hi = DEEPNORM_ALPHA * res[:, HALF_D:]
    for k in range(TOP_K):
        lo, hi = _unpack_pair(ybuf[cur, k])
        w_k = wts[:, k:k + 1]
        z_lo = z_lo + w_k * lo
        z_hi = z_hi + w_k * hi
    h = _layer_norm(jnp.concatenate([z_lo, z_hi], axis=1), g_ref[...], b_ref[...])
    outs[0][...] = h
    if write_bf16:
        outs[1][...] = h.astype(BF16)
    for s in range(n_side):
        whi_ref, wlo_ref, sb_ref = side_refs[3 * s:3 * s + 3]
        outs[1 + int(write_bf16) + s][...] = _dot3(h, whi_ref, wlo_ref) + sb_ref[...]


def _combine_ln(res, y_packed, pos, wts, g, b, side=(), write_bf16=True):
    t = wts.shape[0]
    rows = COMBINE_ROWS
    const = lambda i, *_: (0, 0)
    side_args, side_specs = [], []
    for w_pad, bias in side:
        whi, wlo = _split_hi_lo(w_pad)
        side_args += [whi, wlo, bias.reshape(1, LANES)]
        side_specs += [pl.BlockSpec((D_MODEL, LANES), const), pl.BlockSpec((D_MODEL, LANES), const),
                       pl.BlockSpec((1, LANES), const)]
    n_side = len(side)
    n_wide = 1 + int(write_bf16)
    outs = pl.pallas_call(
        functools.partial(_combine_ln_body, n_side=n_side, write_bf16=write_bf16),
        grid_spec=pltpu.PrefetchScalarGridSpec(
            num_scalar_prefetch=1,
            grid=(t // rows,),
            in_specs=[pl.BlockSpec((rows, D_MODEL), lambda i, *_: (i, 0)),
                      pl.BlockSpec((rows, LANES), lambda i, *_: (i, 0)),
                      pl.BlockSpec((1, D_MODEL), const), pl.BlockSpec((1, D_MODEL), const)]
            + side_specs + [pl.BlockSpec(memory_space=pl.ANY)],
            out_specs=[pl.BlockSpec((rows, D_MODEL), lambda i, *_: (i, 0))] * n_wide
            + [pl.BlockSpec((rows, LANES), lambda i, *_: (i, 0))] * n_side,
            scratch_shapes=[pltpu.VMEM((2, TOP_K, rows, HALF_D), jnp.uint32),
                            pltpu.SemaphoreType.DMA((2,))]),
        out_shape=[jax.ShapeDtypeStruct((t, D_MODEL), F32), jax.ShapeDtypeStruct((t, D_MODEL), BF16)][:n_wide]
        + [jax.ShapeDtypeStruct((t, LANES), F32)] * n_side,
        compiler_params=_params(("arbitrary",)),
        name="moe_combine_ln",
    )(pos, res, wts, g.reshape(1, -1), b.reshape(1, -1), *side_args, y_packed)
    return outs


def _cast_body(x_ref, o_ref):
    o_ref[...] = x_ref[...].astype(o_ref.dtype)


def _cast_layer_bf16(w, layer, name):
    _, e, r, c = w.shape
    rb = min(r, CAST_ROWS * 1024 // c)
    return pl.pallas_call(
        _cast_body,
        grid=(e, r // rb),
        in_specs=[pl.BlockSpec((None, 1, rb, c), lambda i, j: (layer, i, j, 0))],
        out_specs=pl.BlockSpec((1, rb, c), lambda i, j: (i, j, 0)),
        out_shape=jax.ShapeDtypeStruct((e, r, c), BF16),
        compiler_params=_params(("arbitrary", "arbitrary")),
        name=name,
    )(w)


def _moe_block(res, mix, ln_g, ln_b, router_w, router_b, w_gu, b_gu, w_down, b_down, fg, fb, side=(),
               write_bf16=True):
    h, h_packed, ids, wts, cnt = _ln_router(res, mix, ln_g, ln_b, router_w, router_b)
    pos_pad, te_pad, meta = _rank(ids, cnt)
    pos = pos_pad[:, :TOP_K].reshape(-1)
    base, count, pcount = meta[0, :N_EXPERTS], meta[1, :N_EXPERTS], meta[2, :N_EXPERTS]
    n_tiles = (meta[3, N_EXPERTS - 1:N_EXPERTS] // MOE_TM).astype(jnp.int32)
    n_slots = _moe_tiles(res.shape[0]) * MOE_TM
    pcount = pcount.at[N_EXPERTS - 1].set(n_slots - base[N_EXPERTS - 1])
    xs = _dispatch(h_packed, pos, base, count, pcount)
    y = _grouped_ffn(xs, te_pad[:, 0], n_tiles, w_gu, b_gu, w_down, b_down)
    return _combine_ln(h, y, pos, wts, fg, fb, side, write_bf16)


def _compress_body(c_ref, pet_ref, peb_ref, w1t_ref, w1b_ref, b1_ref, w2_ref, b2_ref, cos_ref, sin_ref,
                   o_ref, *, use_rope):
    rows = c_ref.shape[0]
    c = c_ref[...].astype(F32)
    top = jnp.dot((c + pet_ref[...]).astype(BF16), w1t_ref[...], preferred_element_type=F32)
    bot = jnp.dot((c + peb_ref[...]).astype(BF16), w1b_ref[...], preferred_element_type=F32)
    pre = top + pltpu.roll(bot, rows - 1, axis=0) + b1_ref[...]
    out = jnp.dot(jax.nn.gelu(pre).astype(BF16), w2_ref[...], preferred_element_type=F32) + b2_ref[...]
    if use_rope:
        out = out * cos_ref[...] + pltpu.roll(out, HEAD_DIM // 2, axis=1) * sin_ref[...]
    o_ref[...] = out.astype(o_ref.dtype)


def _compress(chunks, pe, w1, b1, w2, b2, cos, sin, use_rope):
    r = chunks.shape[0]
    rows = 4 * N_CHUNK
    half = CMP_STRIDE * HEAD_DIM
    const = lambda i: (0, 0)
    reps = rows // N_CHUNK
    return pl.pallas_call(
        functools.partial(_compress_body, use_rope=use_rope),
        grid=(r // rows,),
        in_specs=[pl.BlockSpec((rows, half), lambda i: (i, 0)),
                  pl.BlockSpec((1, half), const), pl.BlockSpec((1, half), const),
                  pl.BlockSpec((half, CMP_HIDDEN), const), pl.BlockSpec((half, CMP_HIDDEN), const),
                  pl.BlockSpec((1, CMP_HIDDEN), const),
                  pl.BlockSpec((CMP_HIDDEN, HEAD_DIM), const), pl.BlockSpec((1, HEAD_DIM), const),
                  pl.BlockSpec((rows, HEAD_DIM), const), pl.BlockSpec((rows, HEAD_DIM), const)],
        out_specs=pl.BlockSpec((rows, HEAD_DIM), lambda i: (i, 0)),
        out_shape=jax.ShapeDtypeStruct((r, HEAD_DIM), BF16),
        compiler_params=_params(("arbitrary",)),
        name="nsa_compress",
    )(chunks, pe[:CMP_STRIDE].reshape(1, half), pe[CMP_STRIDE:].reshape(1, half),
      w1[:half].astype(BF16), w1[half:].astype(BF16), b1.reshape(1, -1),
      w2.astype(BF16), b2.reshape(1, -1), jnp.tile(cos, (reps, 1)), jnp.tile(sin, (reps, 1)))


def _cmp_attn_body(q_ref, kc_ref, vc_ref, ovt_ref, o_ref, selb_ref):
    s = q_ref.shape[0]
    kc = kc_ref[...]
    vc = vc_ref[...]
    pos = lax.broadcasted_iota(jnp.int32, (s, N_CHUNK), 0)
    cidx = lax.broadcasted_iota(jnp.int32, (s, N_CHUNK), 1)
    mask = (cidx * CMP_STRIDE + (CMP_BLOCK - 1) <= pos) & (cidx < N_CMP)
    psum = jnp.zeros((s, N_CHUNK), F32)
    for h in range(HEADS_PER_GROUP):
        qh = q_ref[:, h * HEAD_DIM:(h + 1) * HEAD_DIM]
        sc = lax.dot_general(qh, kc, (((1,), (1,)), ((), ())), preferred_element_type=F32)
        sc = jnp.where(mask, sc, NEG_BIG)
        m = jnp.max(sc, axis=-1, keepdims=True)
        e = jnp.where(mask, jnp.exp(sc - m), 0.0)
        l = jnp.sum(e, axis=-1, keepdims=True)
        p = e / jnp.where(l > 0.0, l, 1.0)
        psum = psum + p
        o_ref[:, h * HEAD_DIM:(h + 1) * HEAD_DIM] = jnp.dot(
            p.astype(BF16), vc, preferred_element_type=F32).astype(o_ref.dtype)
    ph = psum.astype(BF16)
    pl_ = (psum - ph.astype(F32)).astype(BF16)
    ovt = ovt_ref[...]
    dn = (((1,), (1,)), ((), ()))
    imp = (lax.dot_general(ovt, ph, dn, preferred_element_type=F32)
           + lax.dot_general(ovt, pl_, dn, preferred_element_type=F32))[:N_SLC]
    blk = lax.broadcasted_iota(jnp.int32, (N_SLC, s), 0)
    qpos = lax.broadcasted_iota(jnp.int32, (N_SLC, s), 1)
    cur = qpos // SLC_BLOCK
    causal = blk * SLC_BLOCK <= qpos
    forced = (blk == 0) | (blk == cur) | (blk == cur - 1)
    val = jnp.where(causal, imp + jnp.where(forced, FORCE_BONUS, 0.0), -jnp.inf)
    rank = jnp.zeros((N_SLC, s), jnp.int32)
    for m_ in range(N_SLC):
        row = val[m_:m_ + 1, :]
        tie = jnp.where(blk > m_, 1, 0)
        rank = rank + jnp.where(row > val, 1, jnp.where(row == val, tie, 0))
    sel = causal & (rank < SLC_TOP_N)
    bias = jnp.where(sel, 0.0, NEG_BIG)
    bias = jnp.concatenate([bias, jnp.zeros((LANES - N_SLC, s), F32)], axis=0)
    selb_ref[...] = bias.T.astype(selb_ref.dtype)


def _cmp_attn(q, k_cmp, v_cmp, batch):
    t = batch * SEQ
    gw = HEADS_PER_GROUP * HEAD_DIM
    c0 = jnp.arange(N_CHUNK)[None, :] * CMP_STRIDE
    s0 = jnp.arange(LANES)[:, None] * SLC_BLOCK
    ovt = ((c0 < s0 + SLC_BLOCK) & (c0 + CMP_BLOCK > s0) & (jnp.arange(LANES)[:, None] < N_SLC)
           & (jnp.arange(N_CHUNK)[None, :] < N_CMP)).astype(BF16)
    return pl.pallas_call(
        _cmp_attn_body,
        grid=(batch, N_KV_GROUPS),
        in_specs=[pl.BlockSpec((SEQ, gw), lambda b, g: (b, g)),
                  pl.BlockSpec((N_CHUNK, HEAD_DIM), lambda b, g: (b * N_KV_GROUPS + g, 0)),
                  pl.BlockSpec((N_CHUNK, HEAD_DIM), lambda b, g: (b * N_KV_GROUPS + g, 0)),
                  pl.BlockSpec((LANES, N_CHUNK), lambda b, g: (0, 0))],
        out_specs=[pl.BlockSpec((SEQ, gw), lambda b, g: (b, g)),
                   pl.BlockSpec((SEQ, LANES), lambda b, g: (b * N_KV_GROUPS + g, 0))],
        out_shape=[jax.ShapeDtypeStruct((t, Q_WIDTH), BF16),
                   jax.ShapeDtypeStruct((batch * N_KV_GROUPS * SEQ, LANES), BF16)],
        compiler_params=_params(("arbitrary", "arbitrary")),
        name="nsa_cmp_attn",
    )(q, k_cmp, v_cmp, ovt)


def _softmax_pv(pieces):
    m = functools.reduce(jnp.maximum, [jnp.max(sc, axis=-1, keepdims=True) for sc, _ in pieces])
    o = 0.0
    for sc, v in pieces:
        e = jnp.exp((sc - m).astype(BF16))
        o = o + jnp.dot(e, v, preferred_element_type=F32)
    return o[:, :HEAD_DIM] / o[:, HEAD_DIM:HEAD_DIM + 1]


def _attn_body(q_ref, selb_ref, ks_ref, vsel_ref, kw_ref, vwin_ref, oc_ref, gl_ref, o_ref,
               kaug_scr, vs_ref, vw_ref):
    s = q_ref.shape[0]
    tq = ATT_TQ
    hh = pl.program_id(1) * HEADS_PER_GROUP + pl.program_id(2)
    dn = (((1,), (1,)), ((), ()))
    key = lax.broadcasted_iota(jnp.int32, (s, LANES), 0)
    lane = lax.broadcasted_iota(jnp.int32, (s, LANES), 1)
    kaug_scr[:, :HEAD_DIM] = ks_ref[...]
    kaug_scr[:, HEAD_DIM:] = jnp.where(key // SLC_BLOCK == lane, 1.0, 0.0).astype(BF16)
    ones_col = jnp.where(lane == 0, 1.0, 0.0).astype(BF16)
    vs_ref[:, :HEAD_DIM] = vsel_ref[...]
    vs_ref[:, HEAD_DIM:] = ones_col
    vw_ref[:, :HEAD_DIM] = vwin_ref[...]
    vw_ref[:, HEAD_DIM:] = ones_col
    r_i = lax.broadcasted_iota(jnp.int32, (tq, tq), 0)
    c_i = lax.broadcasted_iota(jnp.int32, (tq, tq), 1)
    causal_bias = jnp.where(c_i <= r_i, 0.0, NEG_BIG)
    band_bias = jnp.where(c_i > r_i, 0.0, NEG_BIG)
    glane = lax.broadcasted_iota(jnp.int32, (tq, LANES), 1)
    for i in range(s // tq):
        q0 = i * tq
        q = q_ref[q0:q0 + tq, :]
        qa = jnp.concatenate([q, selb_ref[q0:q0 + tq, :]], axis=1)
        pieces = []
        if i > 0:
            pieces.append((lax.dot_general(qa, kaug_scr[0:q0, :], dn, preferred_element_type=F32),
                           vs_ref[0:q0, :]))
        pieces.append((lax.dot_general(qa, kaug_scr[q0:q0 + tq, :], dn, preferred_element_type=F32) + causal_bias,
                       vs_ref[q0:q0 + tq, :]))
        o_slc = _softmax_pv(pieces)
        pieces = []
        lo = q0 - WINDOW
        if lo >= 0:
            pieces.append((lax.dot_general(q, kw_ref[lo:lo + tq, :], dn, preferred_element_type=F32) + band_bias,
                           vw_ref[lo:lo + tq, :]))
        mid = max(lo + tq, 0)
        if q0 > mid:
            pieces.append((lax.dot_general(q, kw_ref[mid:q0, :], dn, preferred_element_type=F32),
                           vw_ref[mid:q0, :]))
        pieces.append((lax.dot_general(q, kw_ref[q0:q0 + tq, :], dn, preferred_element_type=F32) + causal_bias,
                       vw_ref[q0:q0 + tq, :]))
        o_win = _softmax_pv(pieces)
        gates = jax.nn.sigmoid(gl_ref[q0:q0 + tq, :])
        gsel = [jnp.sum(jnp.where(glane == hh * N_BRANCHES + br, gates, 0.0), axis=-1, keepdims=True)
                for br in range(N_BRANCHES)]
        o = gsel[0] * oc_ref[q0:q0 + tq, :].astype(F32) + gsel[1] * o_slc + gsel[2] * o_win
        o_ref[q0:q0 + tq, :] = o.astype(o_ref.dtype)


def _attention(q, selb, kv, o_cmp, gate_logits, batch):
    t = batch * SEQ
    g_ = N_KV_GROUPS
    head = lambda b, g, h: (b, g * HEADS_PER_GROUP + h)
    slot = lambda n: (lambda b, g, h: (b, n * g_ + g))
    return pl.pallas_call(
        _attn_body,
        grid=(batch, N_KV_GROUPS, HEADS_PER_GROUP),
        in_specs=[pl.BlockSpec((SEQ, HEAD_DIM), head),
                  pl.BlockSpec((SEQ, LANES), lambda b, g, h: (b * g_ + g, 0)),
                  pl.BlockSpec((SEQ, HEAD_DIM), slot(2)), pl.BlockSpec((SEQ, HEAD_DIM), slot(3)),
                  pl.BlockSpec((SEQ, HEAD_DIM), slot(4)), pl.BlockSpec((SEQ, HEAD_DIM), slot(5)),
                  pl.BlockSpec((SEQ, HEAD_DIM), head),
                  pl.BlockSpec((SEQ, LANES), lambda b, g, h: (b, 0))],
        out_specs=pl.BlockSpec((SEQ, HEAD_DIM), head),
        out_shape=jax.ShapeDtypeStruct((t, Q_WIDTH), BF16),
        scratch_shapes=[pltpu.VMEM((SEQ, 2 * HEAD_DIM), BF16)] * 3,
        compiler_params=_params(("arbitrary", "arbitrary", "arbitrary")),
        name="nsa_attn",
    )(q, selb, kv, kv, kv, kv, o_cmp, gate_logits)


def _nsa_mixer(hb, gate_logits, kv_w, w_in, w_out, cmp_k, cmp_v, batch):
    cos, sin = _rope_tables(jnp.arange(SEQ))
    kv = _matmul_rope(hb, kv_w.astype(BF16), cos, sin, BF16, "nsa_kv_proj", scale=1.0,
                      rope_cols=(2, 4), bm=MM_BM, bn=KV_SLOT_W)
    q = _matmul_rope(hb, w_in[:, :Q_WIDTH].astype(BF16), cos, sin, BF16, "nsa_q_proj",
                     scale=HEAD_DIM ** -0.5, rope_cols=None, bm=MM_BM, bn=MM_BN)

    def chunks(slot):
        c = kv[:, slot * KV_SLOT_W:(slot + 1) * KV_SLOT_W]
        c = c.reshape(batch, N_CHUNK, CMP_STRIDE, N_KV_GROUPS, HEAD_DIM).transpose(0, 3, 1, 2, 4)
        return c.reshape(batch * N_KV_GROUPS * N_CHUNK, CMP_STRIDE * HEAD_DIM)

    ccos, csin = _rope_tables(jnp.arange(N_CHUNK) * CMP_STRIDE + CMP_BLOCK - 1)
    k_cmp = _compress(chunks(0), *cmp_k, ccos, csin, True)
    v_cmp = _compress(chunks(1), *cmp_v, ccos, csin, False)
    o_cmp, selb = _cmp_attn(q, k_cmp, v_cmp, batch)
    o = _attention(q, selb, kv, o_cmp, gate_logits, batch)
    return _matmul(o, w_out.astype(BF16), F32, "nsa_out_proj")


def kernel(x, ln_mix_g, ln_mix_b, ln_ffn_g, ln_ffn_b, a_w_in, a_conv_w, a_conv_b, a_gate_a_w, a_gate_a_b,
           a_gate_x_w, a_gate_x_b, a_lambda, a_w_out, b_w_in, b_w_out, kv_w, cmp_k_pe, cmp_k_w1, cmp_k_b1,
           cmp_k_w2, cmp_k_b2, cmp_v_pe, cmp_v_w1, cmp_v_b1, cmp_v_w2, cmp_v_b2, moe_router_w, moe_router_b,
           moe_w_gu, moe_b_gu, moe_w_down, moe_b_down):
    batch = x.shape[0]
    t = batch * SEQ
    xf = x.reshape(t, D_MODEL)

    def moe(layer, res, mix, side=(), write_bf16=True):
        return _moe_block(res, mix, ln_mix_g[layer], ln_mix_b[layer], moe_router_w[layer], moe_router_b[layer],
                          _cast_layer_bf16(moe_w_gu, layer, "cast_w_gu"), moe_b_gu[layer],
                          _cast_layer_bf16(moe_w_down, layer, "cast_w_down"), moe_b_down[layer],
                          ln_ffn_g[layer], ln_ffn_b[layer], side, write_bf16)

    proj = _matmul(xf.astype(BF16), a_w_in[0].astype(BF16), BF16, "rg_in_proj")
    rec = _rglru(proj, a_conv_w[0], a_conv_b[0], a_gate_a_w[0].astype(BF16), a_gate_a_b[0],
                 a_gate_x_w[0].astype(BF16), a_gate_x_b[0], a_lambda[0], batch)
    mix = _matmul(rec, a_w_out[0].astype(BF16), F32, "rg_out_proj")
    n_gate = N_BRANCHES * N_HEADS
    gate_w = jnp.pad(b_w_in[0][:, Q_WIDTH:], ((0, 0), (0, LANES - n_gate)))
    h, hb, gate_logits = moe(0, xf, mix, side=((gate_w, jnp.zeros((LANES,), F32)),))

    mix = _nsa_mixer(hb, gate_logits, kv_w, b_w_in[0], b_w_out[0],
                     (cmp_k_pe, cmp_k_w1, cmp_k_b1, cmp_k_w2, cmp_k_b2),
                     (cmp_v_pe, cmp_v_w1, cmp_v_b1, cmp_v_w2, cmp_v_b2), batch)
    (h,) = moe(1, h, mix, write_bf16=False)
    return h.reshape(batch, SEQ, D_MODEL)
```

```python
import functools

import jax
import jax.numpy as jnp
from jax import lax
from jax.experimental import pallas as pl
from jax.experimental.pallas import tpu as pltpu

F32 = jnp.float32
BF16 = jnp.bfloat16

D_MODEL = 4096
SEQ = 2048
DEPTH = 2
DEEPNORM_ALPHA = (2 * DEPTH) ** 0.25
LN_EPS = 1e-5
RNN_WIDTH = D_MODEL
RG_BLOCKS = 16
RG_BLOCK_W = RNN_WIDTH // RG_BLOCKS
CONV_WIDTH = 4
RG_C = 8.0
N_HEADS = 32
HEAD_DIM = 128
N_KV_GROUPS = 4
HEADS_PER_GROUP = N_HEADS // N_KV_GROUPS
N_BRANCHES = 3
Q_WIDTH = N_HEADS * HEAD_DIM
KV_SLOT_W = N_KV_GROUPS * HEAD_DIM
KV_WIDTH = 2 * N_BRANCHES * KV_SLOT_W
CMP_BLOCK = 32
CMP_STRIDE = 16
CMP_HIDDEN = 512
N_CMP = (SEQ - CMP_BLOCK) // CMP_STRIDE + 1
N_CHUNK = SEQ // CMP_STRIDE
SLC_BLOCK = 64
N_SLC = SEQ // SLC_BLOCK
SLC_TOP_N = min(16, N_SLC)
WINDOW = 512
ROPE_THETA = 10000.0
NEG_BIG = -1e30
FORCE_BONUS = 1e4
N_EXPERTS = 32
TOP_K = 4
EXPERT_FF = D_MODEL // 8
SWIGLU_ALPHA = 1.702
SWIGLU_LIMIT = 7.0

LANES = 128
SUBLANES = 8
HALF_D = D_MODEL // 2
PACK_TILES = HALF_D // LANES
VMEM_LIMIT = 56 * 1024 * 1024

MM_BM = 1024
MM_BN = 1024
LN_ROWS = 256
COMBINE_ROWS = 128
RANK_ROWS = 512
MOE_TM = 512
FFN_OUT_COLS = 512
CAST_ROWS = 2048
DMA_UNROLL = 4
ATT_TQ = 256


def _params(sem, vmem=VMEM_LIMIT):
    return pltpu.CompilerParams(dimension_semantics=sem, vmem_limit_bytes=vmem)


def _mm_body(x_ref, w_ref, o_ref):
    o_ref[...] = jnp.dot(x_ref[...], w_ref[...], preferred_element_type=F32).astype(o_ref.dtype)


def _matmul(x, w, out_dtype, name, bm=MM_BM, bn=MM_BN):
    m, k = x.shape
    n = w.shape[1]
    return pl.pallas_call(
        _mm_body,
        grid=(n // bn, m // bm),
        in_specs=[pl.BlockSpec((bm, k), lambda j, i: (i, 0)),
                  pl.BlockSpec((k, bn), lambda j, i: (0, j))],
        out_specs=pl.BlockSpec((bm, bn), lambda j, i: (i, j)),
        out_shape=jax.ShapeDtypeStruct((m, n), out_dtype),
        compiler_params=_params(("arbitrary", "arbitrary")),
        name=name,
    )(x, w)


def _mm_rope_body(x_ref, w_ref, cos_ref, sin_ref, o_ref, *, scale, rope_cols):
    acc = jnp.dot(x_ref[...], w_ref[...], preferred_element_type=F32)
    cos = cos_ref[...]
    sin = sin_ref[...]
    if rope_cols is not None:
        j = pl.program_id(0)
        flag = functools.reduce(jnp.logical_or, [j == c for c in rope_cols])
        cos = jnp.where(flag, cos, 1.0)
        sin = jnp.where(flag, sin, 0.0)
    for h in range(acc.shape[1] // HEAD_DIM):
        a = acc[:, h * HEAD_DIM:(h + 1) * HEAD_DIM]
        r = a * cos + pltpu.roll(a, HEAD_DIM // 2, axis=1) * sin
        o_ref[:, h * HEAD_DIM:(h + 1) * HEAD_DIM] = (r * scale).astype(o_ref.dtype)


def _matmul_rope(x, w, cos, sin, out_dtype, name, *, scale, rope_cols, bm, bn):
    m, k = x.shape
    n = w.shape[1]
    s_blocks = SEQ // bm
    return pl.pallas_call(
        functools.partial(_mm_rope_body, scale=scale, rope_cols=rope_cols),
        grid=(n // bn, m // bm),
        in_specs=[pl.BlockSpec((bm, k), lambda j, i: (i, 0)),
                  pl.BlockSpec((k, bn), lambda j, i: (0, j)),
                  pl.BlockSpec((bm, HEAD_DIM), lambda j, i: (i % s_blocks, 0)),
                  pl.BlockSpec((bm, HEAD_DIM), lambda j, i: (i % s_blocks, 0))],
        out_specs=pl.BlockSpec((bm, bn), lambda j, i: (i, j)),
        out_shape=jax.ShapeDtypeStruct((m, n), out_dtype),
        compiler_params=_params(("arbitrary", "arbitrary")),
        name=name,
    )(x, w, cos, sin)


def _rope_tables(pos):
    half = HEAD_DIM // 2
    inv = ROPE_THETA ** (-jnp.arange(half, dtype=F32) / half)
    ang = pos.astype(F32)[:, None] * inv[None, :]
    cos = jnp.cos(ang)
    sin = jnp.sin(ang)
    return jnp.concatenate([cos, cos], axis=1), jnp.concatenate([-sin, sin], axis=1)


def _sigmoid(x):
    return 0.5 * jnp.tanh(0.5 * x) + 0.5


def _rglru_body(y_ref, u_ref, cw_ref, cb_ref, wa_ref, ba_ref, wx_ref, bx_ref, lam_ref, o_ref,
                a_scr, b_scr):
    s = u_ref.shape[0]
    u_in = u_ref[...].astype(F32)
    row = lax.broadcasted_iota(jnp.int32, u_in.shape, 0)
    u = u_in * cw_ref[CONV_WIDTH - 1:CONV_WIDTH, :] + cb_ref[...]
    for d in range(1, CONV_WIDTH):
        shifted = jnp.where(row >= d, pltpu.roll(u_in, d, axis=0), 0.0)
        u = u + shifted * cw_ref[CONV_WIDTH - 1 - d:CONV_WIDTH - d, :]
    ub = u.astype(BF16)
    gate_x = _sigmoid(jnp.dot(ub, wx_ref[0], preferred_element_type=F32) + bx_ref[0])
    gate_a = _sigmoid(jnp.dot(ub, wa_ref[0], preferred_element_type=F32) + ba_ref[0])
    z = -lam_ref[...]
    softplus = jnp.maximum(z, 0.0) + jnp.log1p(jnp.exp(-jnp.abs(z)))
    log_a = -RG_C * gate_a * softplus
    a = jnp.exp(log_a)
    b = jnp.sqrt(1.0 - a * a) * (gate_x * u)
    width = u_in.shape[1]
    a = a.reshape(s // SUBLANES, SUBLANES, width)
    b = b.reshape(s // SUBLANES, SUBLANES, width)
    sub = lax.broadcasted_iota(jnp.int32, a.shape, 1)
    for d in (1, 2, 4):
        valid = sub >= d
        a_prev = pltpu.roll(a, d, axis=1)
        b_prev = pltpu.roll(b, d, axis=1)
        b = jnp.where(valid, a * b_prev + b, b)
        a = jnp.where(valid, a * a_prev, a)
    a_scr[...] = a.reshape(s, width)
    b_scr[...] = b.reshape(s, width)

    def step(g, carry):
        r0 = pl.multiple_of(g * SUBLANES, SUBLANES)
        hs = a_scr[pl.ds(r0, SUBLANES), :] * carry + b_scr[pl.ds(r0, SUBLANES), :]
        b_scr[pl.ds(r0, SUBLANES), :] = hs
        return jnp.broadcast_to(hs[SUBLANES - 1:SUBLANES, :], hs.shape)

    lax.fori_loop(0, s // SUBLANES, step, jnp.zeros((SUBLANES, u_in.shape[1]), F32))
    y = jax.nn.gelu(y_ref[...].astype(F32))
    o_ref[...] = (b_scr[...] * y).astype(o_ref.dtype)


def _rglru(proj, conv_w, conv_b, gate_a_w, gate_a_b, gate_x_w, gate_x_b, lam, batch):
    w = RG_BLOCK_W
    t = batch * SEQ
    vec = lambda b, n: (0, n)
    blk = lambda b, n: (n, 0, 0)
    return pl.pallas_call(
        _rglru_body,
        grid=(batch, RG_BLOCKS),
        in_specs=[pl.BlockSpec((SEQ, w), lambda b, n: (b, n)),
                  pl.BlockSpec((SEQ, w), lambda b, n: (b, RG_BLOCKS + n)),
                  pl.BlockSpec((CONV_WIDTH, w), vec),
                  pl.BlockSpec((1, w), vec),
                  pl.BlockSpec((1, w, w), blk), pl.BlockSpec((1, 1, w), blk),
                  pl.BlockSpec((1, w, w), blk), pl.BlockSpec((1, 1, w), blk),
                  pl.BlockSpec((1, w), vec)],
        out_specs=pl.BlockSpec((SEQ, w), lambda b, n: (b, n)),
        out_shape=jax.ShapeDtypeStruct((t, RNN_WIDTH), BF16),
        scratch_shapes=[pltpu.VMEM((SEQ, w), F32), pltpu.VMEM((SEQ, w), F32)],
        compiler_params=_params(("arbitrary", "arbitrary")),
        name="rglru",
    )(proj, proj, conv_w, conv_b.reshape(1, -1),
      gate_a_w, gate_a_b.reshape(RG_BLOCKS, 1, w), gate_x_w, gate_x_b.reshape(RG_BLOCKS, 1, w),
      lam.reshape(1, -1))


def _split_hi_lo(w):
    hi = w.astype(BF16)
    lo = (w - hi.astype(F32)).astype(BF16)
    return hi, lo


def _dot3(x, whi_ref, wlo_ref):
    xh = x.astype(BF16)
    xl = (x - xh.astype(F32)).astype(BF16)
    whi = whi_ref[...]
    return (jnp.dot(xh, whi, preferred_element_type=F32)
            + jnp.dot(xl, whi, preferred_element_type=F32)
            + jnp.dot(xh, wlo_ref[...], preferred_element_type=F32))


def _layer_norm(z, g, b):
    mu = jnp.mean(z, axis=-1, keepdims=True)
    zc = z - mu
    var = jnp.mean(zc * zc, axis=-1, keepdims=True)
    return zc * lax.rsqrt(var + LN_EPS) * g + b


def _lane_pack(cols, lane):
    out = jnp.broadcast_to(cols[-1], lane.shape)
    for k in range(len(cols) - 2, -1, -1):
        out = jnp.where(lane == k, cols[k], out)
    return out


def _pack_pair(lo, hi):
    lo_bits = pltpu.bitcast(lo.astype(BF16).astype(F32), jnp.uint32) >> 16
    hi_bits = pltpu.bitcast(hi.astype(BF16).astype(F32), jnp.uint32) & jnp.uint32(0xFFFF0000)
    return lo_bits | hi_bits


def _unpack_pair(words):
    lo = pltpu.bitcast(words << 16, F32)
    hi = pltpu.bitcast(words & jnp.uint32(0xFFFF0000), F32)
    return lo, hi


def _rows_to_tiles(words):
    groups = jnp.stack([words[:, j * LANES:(j + 1) * LANES] for j in range(PACK_TILES)], axis=0)
    return jnp.swapaxes(groups, 0, 1)


def _tiles_to_rows(tiles):
    groups = jnp.swapaxes(tiles, 0, 1)
    return jnp.concatenate([groups[j] for j in range(PACK_TILES)], axis=1)


def _ln_router_body(res_ref, mix_ref, g_ref, b_ref, whi_ref, wlo_ref, rb_ref,
                    h_ref, hp_ref, ids_ref, wts_ref, cnt_ref):
    z = DEEPNORM_ALPHA * res_ref[...] + mix_ref[...].astype(F32)
    h = _layer_norm(z, g_ref[...], b_ref[...])
    h_ref[...] = h
    hp_ref[...] = _rows_to_tiles(_pack_pair(h[:, :HALF_D], h[:, HALF_D:]))
    logits = _dot3(h, whi_ref, wlo_ref) + rb_ref[...]
    lane = lax.broadcasted_iota(jnp.int32, logits.shape, 1)
    cur = jnp.where(lane < N_EXPERTS, logits, -jnp.inf)
    vals, idxs = [], []
    for _ in range(TOP_K):
        m = jnp.max(cur, axis=-1, keepdims=True)
        idx = jnp.min(jnp.where(cur == m, lane, LANES), axis=-1, keepdims=True)
        vals.append(m)
        idxs.append(idx)
        cur = jnp.where(lane == idx, -jnp.inf, cur)
    es = [jnp.exp(v - vals[0]) for v in vals]
    tot = functools.reduce(lambda p, q: p + q, es)
    ids_ref[...] = _lane_pack(idxs, lane)
    wts_ref[...] = _lane_pack([e / tot for e in es], lane)
    hot = functools.reduce(lambda p, q: p + q, [(lane == i).astype(F32) for i in idxs])

    @pl.when(pl.program_id(0) == 0)
    def _():
        cnt_ref[...] = jnp.zeros_like(cnt_ref)

    cnt_ref[...] += jnp.broadcast_to(jnp.sum(hot, axis=0, keepdims=True), cnt_ref.shape)


def _ln_router(res, mix, g, b, router_w, router_b):
    t = res.shape[0]
    rows = LN_ROWS
    w_pad = jnp.pad(router_w, ((0, 0), (0, LANES - N_EXPERTS)))
    whi, wlo = _split_hi_lo(w_pad)
    rb = jnp.pad(router_b, (0, LANES - N_EXPERTS)).reshape(1, LANES)
    const = lambda i: (0, 0)
    return pl.pallas_call(
        _ln_router_body,
        grid=(t // rows,),
        in_specs=[pl.BlockSpec((rows, D_MODEL), lambda i: (i, 0)),
                  pl.BlockSpec((rows, D_MODEL), lambda i: (i, 0)),
                  pl.BlockSpec((1, D_MODEL), const), pl.BlockSpec((1, D_MODEL), const),
                  pl.BlockSpec((D_MODEL, LANES), const), pl.BlockSpec((D_MODEL, LANES), const),
                  pl.BlockSpec((1, LANES), const)],
        out_specs=[pl.BlockSpec((rows, D_MODEL), lambda i: (i, 0)),
                   pl.BlockSpec((rows, PACK_TILES, LANES), lambda i: (i, 0, 0)),
                   pl.BlockSpec((rows, LANES), lambda i: (i, 0)),
                   pl.BlockSpec((rows, LANES), lambda i: (i, 0)),
                   pl.BlockSpec((SUBLANES, LANES), const)],
        out_shape=[jax.ShapeDtypeStruct((t, D_MODEL), F32),
                   jax.ShapeDtypeStruct((t, PACK_TILES, LANES), jnp.uint32),
                   jax.ShapeDtypeStruct((t, LANES), jnp.int32),
                   jax.ShapeDtypeStruct((t, LANES), F32),
                   jax.ShapeDtypeStruct((SUBLANES, LANES), F32)],
        compiler_params=_params(("arbitrary",)),
        name="ln_router",
    )(res, mix, g.reshape(1, -1), b.reshape(1, -1), whi, wlo, rb)


def _moe_tiles(t):
    return (t * TOP_K) // MOE_TM + N_EXPERTS


def _rank_body(ids_ref, cnt_ref, pos_ref, te_ref, meta_ref, carry_scr, base_scr):
    rows = ids_ref.shape[0]
    lane8 = lax.broadcasted_iota(jnp.int32, (SUBLANES, LANES), 1)

    @pl.when(pl.program_id(0) == 0)
    def _():
        cnt = cnt_ref[...]
        padded = jnp.floor((cnt + (MOE_TM - 1)) / MOE_TM) * MOE_TM
        padded = jnp.where(lane8 < N_EXPERTS, padded, 0.0)
        ends = padded
        d = 1
        while d < N_EXPERTS:
            ends = ends + jnp.where(lane8 >= d, pltpu.roll(ends, d, axis=1), 0.0)
            d *= 2
        base = ends - padded
        base_scr[...] = base
        carry_scr[...] = jnp.zeros_like(carry_scr)
        sub8 = lax.broadcasted_iota(jnp.int32, (SUBLANES, LANES), 0)
        meta = jnp.where(sub8 == 0, base, jnp.where(sub8 == 1, cnt, jnp.where(sub8 == 2, padded, ends)))
        meta_ref[...] = meta.astype(jnp.int32)
        tstart = lax.broadcasted_iota(jnp.int32, te_ref.shape, 0).astype(F32) * MOE_TM
        lane_t = lax.broadcasted_iota(jnp.int32, te_ref.shape, 1)
        done = jnp.where(lane_t < N_EXPERTS, (ends[0:1, :] <= tstart).astype(F32), 0.0)
        te = jnp.minimum(jnp.sum(done, axis=-1, keepdims=True), N_EXPERTS - 1.0)
        te_ref[...] = jnp.broadcast_to(te, te_ref.shape).astype(jnp.int32)

    ids = ids_ref[...]
    lane = lax.broadcasted_iota(jnp.int32, ids.shape, 1)
    hits = [ids[:, k:k + 1] == lane for k in range(TOP_K)]
    hot = functools.reduce(lambda p, q: p + q, [h.astype(F32) for h in hits])
    r_i = lax.broadcasted_iota(jnp.int32, (rows, rows), 0)
    c_i = lax.broadcasted_iota(jnp.int32, (rows, rows), 1)
    lower = jnp.where(r_i > c_i, 1.0, 0.0).astype(BF16)
    before = jnp.dot(lower, hot.astype(BF16), preferred_element_type=F32)
    slot = before + carry_scr[0:1, :] + base_scr[0:1, :]
    cols = [jnp.sum(jnp.where(h, slot, 0.0), axis=-1, keepdims=True) for h in hits]
    pos_ref[...] = _lane_pack(cols, lane).astype(jnp.int32)
    carry_scr[...] += jnp.broadcast_to(jnp.sum(hot, axis=0, keepdims=True), carry_scr.shape)


def _rank(ids, cnt):
    t = ids.shape[0]
    rows = RANK_ROWS
    nt_pad = -(-_moe_tiles(t) // SUBLANES) * SUBLANES
    const = lambda i: (0, 0)
    return pl.pallas_call(
        _rank_body,
        grid=(t // rows,),
        in_specs=[pl.BlockSpec((rows, LANES), lambda i: (i, 0)),
                  pl.BlockSpec((SUBLANES, LANES), const)],
        out_specs=[pl.BlockSpec((rows, LANES), lambda i: (i, 0)),
                   pl.BlockSpec((nt_pad, LANES), const),
                   pl.BlockSpec((SUBLANES, LANES), const)],
        out_shape=[jax.ShapeDtypeStruct((t, LANES), jnp.int32),
                   jax.ShapeDtypeStruct((nt_pad, LANES), jnp.int32),
                   jax.ShapeDtypeStruct((SUBLANES, LANES), jnp.int32)],
        scratch_shapes=[pltpu.VMEM((SUBLANES, LANES), F32), pltpu.VMEM((SUBLANES, LANES), F32)],
        compiler_params=_params(("arbitrary",)),
        name="moe_rank",
    )(ids, cnt)


def _dispatch_body(pos_ref, base_ref, cnt_ref, pcnt_ref, h_ref, xs_ref, zero_scr, sem):
    i = pl.program_id(0)
    rows = h_ref.shape[0]

    def row_copy(src_row_ref, slot):
        return pltpu.make_async_copy(src_row_ref, xs_ref.at[slot], sem)

    @pl.when(i == 0)
    def _():
        zero_scr[...] = jnp.zeros_like(zero_scr)

        def per_expert(e, _):
            def start(r, _):
                row_copy(zero_scr, base_ref[e] + r).start()
                return 0

            def wait(r, _):
                row_copy(zero_scr, base_ref[e] + r).wait()
                return 0

            lax.fori_loop(cnt_ref[e], pcnt_ref[e], start, 0)
            lax.fori_loop(cnt_ref[e], pcnt_ref[e], wait, 0)
            return 0

        lax.fori_loop(0, N_EXPERTS, per_expert, 0)

    def start(t, _):
        for k in range(TOP_K):
            row_copy(h_ref.at[t], pos_ref[(i * rows + t) * TOP_K + k]).start()
        return 0

    lax.fori_loop(0, rows, start, 0, unroll=DMA_UNROLL)
    for _ in range(TOP_K):
        pltpu.make_async_copy(h_ref, xs_ref.at[pl.ds(0, rows)], sem).wait()


def _dispatch(h_packed, pos, base, cnt, pcnt):
    t = h_packed.shape[0]
    rows = LN_ROWS
    p_pad = _moe_tiles(t) * MOE_TM
    return pl.pallas_call(
        _dispatch_body,
        grid_spec=pltpu.PrefetchScalarGridSpec(
            num_scalar_prefetch=4,
            grid=(t // rows,),
            in_specs=[pl.BlockSpec((rows, PACK_TILES, LANES), lambda i, *_: (i, 0, 0))],
            out_specs=pl.BlockSpec(memory_space=pl.ANY),
            scratch_shapes=[pltpu.VMEM((PACK_TILES, LANES), jnp.uint32), pltpu.SemaphoreType.DMA(())]),
        out_shape=jax.ShapeDtypeStruct((p_pad, PACK_TILES, LANES), jnp.uint32),
        compiler_params=_params(("arbitrary",)),
        name="moe_dispatch",
    )(pos, base, cnt, pcnt, h_packed)


def _ffn_body(te_ref, nt_ref, xs_ref, wgu_ref, bgu_ref, wd_ref, bd_ref, y_ref):
    @pl.when(pl.program_id(0) < nt_ref[0])
    def _():
        lo, hi = _unpack_pair(_tiles_to_rows(xs_ref[...]))
        x = jnp.concatenate([lo.astype(BF16), hi.astype(BF16)], axis=1)
        gu = jnp.dot(x, wgu_ref[0], preferred_element_type=F32) + bgu_ref[0]
        gate = jnp.minimum(gu[:, :EXPERT_FF], SWIGLU_LIMIT)
        up = jnp.clip(gu[:, EXPERT_FF:], -SWIGLU_LIMIT, SWIGLU_LIMIT)
        act = ((up + 1.0) * gate * _sigmoid(SWIGLU_ALPHA * gate)).astype(BF16)
        packed = []
        for c in range(0, HALF_D, FFN_OUT_COLS):
            d = c + HALF_D
            y_lo = jnp.dot(act, wd_ref[0, :, c:c + FFN_OUT_COLS], preferred_element_type=F32)
            y_hi = jnp.dot(act, wd_ref[0, :, d:d + FFN_OUT_COLS], preferred_element_type=F32)
            packed.append(_pack_pair(y_lo + bd_ref[0, :, c:c + FFN_OUT_COLS],
                                     y_hi + bd_ref[0, :, d:d + FFN_OUT_COLS]))
        y_ref[...] = _rows_to_tiles(jnp.concatenate(packed, axis=1))

    @pl.when(pl.program_id(0) >= nt_ref[0])
    def _():
        y_ref[...] = jnp.zeros_like(y_ref)


def _grouped_ffn(xs, tile_expert, n_tiles, w_gu, b_gu, w_down, b_down):
    p_pad = xs.shape[0]
    tm = MOE_TM
    row_blk = lambda i, te, nt: (jnp.minimum(i, nt[0] - 1), 0, 0)
    exp_blk = lambda i, te, nt: (te[i], 0, 0)
    return pl.pallas_call(
        _ffn_body,
        grid_spec=pltpu.PrefetchScalarGridSpec(
            num_scalar_prefetch=2,
            grid=(p_pad // tm,),
            in_specs=[pl.BlockSpec((tm, PACK_TILES, LANES), row_blk),
                      pl.BlockSpec((1, D_MODEL, 2 * EXPERT_FF), exp_blk),
                      pl.BlockSpec((1, 1, 2 * EXPERT_FF), exp_blk),
                      pl.BlockSpec((1, EXPERT_FF, D_MODEL), exp_blk),
                      pl.BlockSpec((1, 1, D_MODEL), exp_blk)],
            out_specs=pl.BlockSpec((tm, PACK_TILES, LANES), lambda i, te, nt: (i, 0, 0))),
        out_shape=jax.ShapeDtypeStruct(xs.shape, jnp.uint32),
        compiler_params=_params(("arbitrary",)),
        name="moe_ffn",
    )(tile_expert, n_tiles, xs, w_gu, b_gu.reshape(N_EXPERTS, 1, -1), w_down,
      b_down.reshape(N_EXPERTS, 1, -1))


def _combine_ln_body(pos_ref, res_ref, wts_ref, g_ref, b_ref, *rest, n_side, write_bf16):
    side_refs = rest[:3 * n_side]
    y_hbm = rest[3 * n_side]
    outs = rest[3 * n_side + 1:-2]
    ybuf, sem = rest[-2:]
    i = pl.program_id(0)
    n_steps = pl.num_programs(0)
    rows = wts_ref.shape[0]
    cur = i % 2

    def row_copy(step, buf, t, k):
        slot = pos_ref[(step * rows + t) * TOP_K + k]
        return pltpu.make_async_copy(y_hbm.at[slot], ybuf.at[buf, k, t], sem.at[buf])

    def issue(step, buf):
        def body(t, _):
            for k in range(TOP_K):
                row_copy(step, buf, t, k).start()
            return 0

        lax.fori_loop(0, rows, body, 0, unroll=DMA_UNROLL)

    @pl.when(i == 0)
    def _():
        issue(0, 0)

    @pl.when(i + 1 < n_steps)
    def _():
        issue(i + 1, 1 - cur)

    for k in range(TOP_K):
        pltpu.make_async_copy(y_hbm.at[pl.ds(0, rows)], ybuf.at[cur, k], sem.at[cur]).wait()

    wts = wts_ref[...]
    res = res_ref[...]
    z_lo = DEEPNORM_ALPHA * res[:, :HALF_D]
    z_hi = DEEPNORM_ALPHA * res[:, HALF_D:]
    for k in range(TOP_K):
        lo, hi = _unpack_pair(_tiles_to_rows(ybuf[cur, k]))
        w_k = wts[:, k:k + 1]
        z_lo = z_lo + w_k * lo
        z_hi = z_hi + w_k * hi
    h = _layer_norm(jnp.concatenate([z_lo, z_hi], axis=1), g_ref[...], b_ref[...])
    outs[0][...] = h
    if write_bf16:
        outs[1][...] = h.astype(BF16)
    for s in range(n_side):
        whi_ref, wlo_ref, sb_ref = side_refs[3 * s:3 * s + 3]
        outs[1 + int(write_bf16) + s][...] = _dot3(h, whi_ref, wlo_ref) + sb_ref[...]


def _combine_ln(res, y_packed, pos, wts, g, b, side=(), write_bf16=True):
    t = wts.shape[0]
    rows = COMBINE_ROWS
    const = lambda i, *_: (0, 0)
    side_args, side_specs = [], []
    for w_pad, bias in side:
        whi, wlo = _split_hi_lo(w_pad)
        side_args += [whi, wlo, bias.reshape(1, LANES)]
        side_specs += [pl.BlockSpec((D_MODEL, LANES), const), pl.BlockSpec((D_MODEL, LANES), const),
                       pl.BlockSpec((1, LANES), const)]
    n_side = len(side)
    n_wide = 1 + int(write_bf16)
    outs = pl.pallas_call(
        functools.partial(_combine_ln_body, n_side=n_side, write_bf16=write_bf16),
        grid_spec=pltpu.PrefetchScalarGridSpec(
            num_scalar_prefetch=1,
            grid=(t // rows,),
            in_specs=[pl.BlockSpec((rows, D_MODEL), lambda i, *_: (i, 0)),
                      pl.BlockSpec((rows, LANES), lambda i, *_: (i, 0)),
                      pl.BlockSpec((1, D_MODEL), const), pl.BlockSpec((1, D_MODEL), const)]
            + side_specs + [pl.BlockSpec(memory_space=pl.ANY)],
            out_specs=[pl.BlockSpec((rows, D_MODEL), lambda i, *_: (i, 0))] * n_wide
            + [pl.BlockSpec((rows, LANES), lambda i, *_: (i, 0))] * n_side,
            scratch_shapes=[pltpu.VMEM((2, TOP_K, rows, PACK_TILES, LANES), jnp.uint32),
                            pltpu.SemaphoreType.DMA((2,))]),
        out_shape=[jax.ShapeDtypeStruct((t, D_MODEL), F32), jax.ShapeDtypeStruct((t, D_MODEL), BF16)][:n_wide]
        + [jax.ShapeDtypeStruct((t, LANES), F32)] * n_side,
        compiler_params=_params(("arbitrary",)),
        name="moe_combine_ln",
    )(pos, res, wts, g.reshape(1, -1), b.reshape(1, -1), *side_args, y_packed)
    return outs


def _cast_body(x_ref, o_ref):
    o_ref[...] = x_ref[...].astype(o_ref.dtype)


def _cast_layer_bf16(w, layer, name):
    _, e, r, c = w.shape
    rb = min(r, CAST_ROWS * 1024 // c)
    return pl.pallas_call(
        _cast_body,
        grid=(e, r // rb),
        in_specs=[pl.BlockSpec((None, 1, rb, c), lambda i, j: (layer, i, j, 0))],
        out_specs=pl.BlockSpec((1, rb, c), lambda i, j: (i, j, 0)),
        out_shape=jax.ShapeDtypeStruct((e, r, c), BF16),
        compiler_params=_params(("arbitrary", "arbitrary")),
        name=name,
    )(w)


def _moe_block(res, mix, ln_g, ln_b, router_w, router_b, w_gu, b_gu, w_down, b_down, fg, fb, side=(),
               write_bf16=True):
    h, h_packed, ids, wts, cnt = _ln_router(res, mix, ln_g, ln_b, router_w, router_b)
    pos_pad, te_pad, meta = _rank(ids, cnt)
    pos = pos_pad[:, :TOP_K].reshape(-1)
    base, count, pcount = meta[0, :N_EXPERTS], meta[1, :N_EXPERTS], meta[2, :N_EXPERTS]
    n_tiles = (meta[3, N_EXPERTS - 1:N_EXPERTS] // MOE_TM).astype(jnp.int32)
    n_slots = _moe_tiles(res.shape[0]) * MOE_TM
    pcount = pcount.at[N_EXPERTS - 1].set(n_slots - base[N_EXPERTS - 1])
    xs = _dispatch(h_packed, pos, base, count, pcount)
    y = _grouped_ffn(xs, te_pad[:, 0], n_tiles, w_gu, b_gu, w_down, b_down)
    return _combine_ln(h, y, pos, wts, fg, fb, side, write_bf16)


def _compress_body(c_ref, pet_ref, peb_ref, w1t_ref, w1b_ref, b1_ref, w2_ref, b2_ref, cos_ref, sin_ref,
                   o_ref, *, use_rope):
    rows = c_ref.shape[0]
    c = c_ref[...].astype(F32)
    top = jnp.dot((c + pet_ref[...]).astype(BF16), w1t_ref[...], preferred_element_type=F32)
    bot = jnp.dot((c + peb_ref[...]).astype(BF16), w1b_ref[...], preferred_element_type=F32)
    pre = top + pltpu.roll(bot, rows - 1, axis=0) + b1_ref[...]
    out = jnp.dot(jax.nn.gelu(pre).astype(BF16), w2_ref[...], preferred_element_type=F32) + b2_ref[...]
    if use_rope:
        out = out * cos_ref[...] + pltpu.roll(out, HEAD_DIM // 2, axis=1) * sin_ref[...]
    o_ref[...] = out.astype(o_ref.dtype)


def _compress(chunks, pe, w1, b1, w2, b2, cos, sin, use_rope):
    r = chunks.shape[0]
    rows = 4 * N_CHUNK
    half = CMP_STRIDE * HEAD_DIM
    const = lambda i: (0, 0)
    reps = rows // N_CHUNK
    return pl.pallas_call(
        functools.partial(_compress_body, use_rope=use_rope),
        grid=(r // rows,),
        in_specs=[pl.BlockSpec((rows, half), lambda i: (i, 0)),
                  pl.BlockSpec((1, half), const), pl.BlockSpec((1, half), const),
                  pl.BlockSpec((half, CMP_HIDDEN), const), pl.BlockSpec((half, CMP_HIDDEN), const),
                  pl.BlockSpec((1, CMP_HIDDEN), const),
                  pl.BlockSpec((CMP_HIDDEN, HEAD_DIM), const), pl.BlockSpec((1, HEAD_DIM), const),
                  pl.BlockSpec((rows, HEAD_DIM), const), pl.BlockSpec((rows, HEAD_DIM), const)],
        out_specs=pl.BlockSpec((rows, HEAD_DIM), lambda i: (i, 0)),
        out_shape=jax.ShapeDtypeStruct((r, HEAD_DIM), BF16),
        compiler_params=_params(("arbitrary",)),
        name="nsa_compress",
    )(chunks, pe[:CMP_STRIDE].reshape(1, half), pe[CMP_STRIDE:].reshape(1, half),
      w1[:half].astype(BF16), w1[half:].astype(BF16), b1.reshape(1, -1),
      w2.astype(BF16), b2.reshape(1, -1), jnp.tile(cos, (reps, 1)), jnp.tile(sin, (reps, 1)))


def _cmp_attn_body(q_ref, kc_ref, vc_ref, ovt_ref, o_ref, selb_ref):
    s = q_ref.shape[0]
    kc = kc_ref[...]
    vc = vc_ref[...]
    pos = lax.broadcasted_iota(jnp.int32, (s, N_CHUNK), 0)
    cidx = lax.broadcasted_iota(jnp.int32, (s, N_CHUNK), 1)
    mask = (cidx * CMP_STRIDE + (CMP_BLOCK - 1) <= pos) & (cidx < N_CMP)
    psum = jnp.zeros((s, N_CHUNK), F32)
    for h in range(HEADS_PER_GROUP):
        qh = q_ref[:, h * HEAD_DIM:(h + 1) * HEAD_DIM]
        sc = lax.dot_general(qh, kc, (((1,), (1,)), ((), ())), preferred_element_type=F32)
        sc = jnp.where(mask, sc, NEG_BIG)
        m = jnp.max(sc, axis=-1, keepdims=True)
        e = jnp.where(mask, jnp.exp(sc - m), 0.0)
        l = jnp.sum(e, axis=-1, keepdims=True)
        p = e / jnp.where(l > 0.0, l, 1.0)
        psum = psum + p
        o_ref[:, h * HEAD_DIM:(h + 1) * HEAD_DIM] = jnp.dot(
            p.astype(BF16), vc, preferred_element_type=F32).astype(o_ref.dtype)
    ph = psum.astype(BF16)
    pl_ = (psum - ph.astype(F32)).astype(BF16)
    ovt = ovt_ref[...]
    dn = (((1,), (1,)), ((), ()))
    imp = (lax.dot_general(ovt, ph, dn, preferred_element_type=F32)
           + lax.dot_general(ovt, pl_, dn, preferred_element_type=F32))[:N_SLC]
    blk = lax.broadcasted_iota(jnp.int32, (N_SLC, s), 0)
    qpos = lax.broadcasted_iota(jnp.int32, (N_SLC, s), 1)
    cur = qpos // SLC_BLOCK
    causal = blk * SLC_BLOCK <= qpos
    forced = (blk == 0) | (blk == cur) | (blk == cur - 1)
    val = jnp.where(causal, imp + jnp.where(forced, FORCE_BONUS, 0.0), -jnp.inf)
    rank = jnp.zeros((N_SLC, s), jnp.int32)
    for m_ in range(N_SLC):
        row = val[m_:m_ + 1, :]
        tie = jnp.where(blk > m_, 1, 0)
        rank = rank + jnp.where(row > val, 1, jnp.where(row == val, tie, 0))
    sel = causal & (rank < SLC_TOP_N)
    bias = jnp.where(sel, 0.0, NEG_BIG)
    bias = jnp.concatenate([bias, jnp.zeros((LANES - N_SLC, s), F32)], axis=0)
    selb_ref[...] = bias.T.astype(selb_ref.dtype)


def _cmp_attn(q, k_cmp, v_cmp, batch):
    t = batch * SEQ
    gw = HEADS_PER_GROUP * HEAD_DIM
    c0 = jnp.arange(N_CHUNK)[None, :] * CMP_STRIDE
    s0 = jnp.arange(LANES)[:, None] * SLC_BLOCK
    ovt = ((c0 < s0 + SLC_BLOCK) & (c0 + CMP_BLOCK > s0) & (jnp.arange(LANES)[:, None] < N_SLC)
           & (jnp.arange(N_CHUNK)[None, :] < N_CMP)).astype(BF16)
    return pl.pallas_call(
        _cmp_attn_body,
        grid=(batch, N_KV_GROUPS),
        in_specs=[pl.BlockSpec((SEQ, gw), lambda b, g: (b, g)),
                  pl.BlockSpec((N_CHUNK, HEAD_DIM), lambda b, g: (b * N_KV_GROUPS + g, 0)),
                  pl.BlockSpec((N_CHUNK, HEAD_DIM), lambda b, g: (b * N_KV_GROUPS + g, 0)),
                  pl.BlockSpec((LANES, N_CHUNK), lambda b, g: (0, 0))],
        out_specs=[pl.BlockSpec((SEQ, gw), lambda b, g: (b, g)),
                   pl.BlockSpec((SEQ, LANES), lambda b, g: (b * N_KV_GROUPS + g, 0))],
        out_shape=[jax.ShapeDtypeStruct((t, Q_WIDTH), BF16),
                   jax.ShapeDtypeStruct((batch * N_KV_GROUPS * SEQ, LANES), BF16)],
        compiler_params=_params(("arbitrary", "arbitrary")),
        name="nsa_cmp_attn",
    )(q, k_cmp, v_cmp, ovt)


def _softmax_pv(pieces):
    m = functools.reduce(jnp.maximum, [jnp.max(sc, axis=-1, keepdims=True) for sc, _ in pieces])
    o = 0.0
    for sc, v in pieces:
        e = jnp.exp((sc - m).astype(BF16))
        o = o + jnp.dot(e, v, preferred_element_type=F32)
    return o[:, :HEAD_DIM] / o[:, HEAD_DIM:HEAD_DIM + 1]


def _attn_body(q_ref, selb_ref, ks_ref, vsel_ref, kw_ref, vwin_ref, oc_ref, gl_ref, o_ref,
               kaug_scr, vs_ref, vw_ref):
    s = q_ref.shape[0]
    tq = ATT_TQ
    hh = pl.program_id(1) * HEADS_PER_GROUP + pl.program_id(2)
    dn = (((1,), (1,)), ((), ()))
    key = lax.broadcasted_iota(jnp.int32, (s, LANES), 0)
    lane = lax.broadcasted_iota(jnp.int32, (s, LANES), 1)
    kaug_scr[:, :HEAD_DIM] = ks_ref[...]
    kaug_scr[:, HEAD_DIM:] = jnp.where(key // SLC_BLOCK == lane, 1.0, 0.0).astype(BF16)
    ones_col = jnp.where(lane == 0, 1.0, 0.0).astype(BF16)
    vs_ref[:, :HEAD_DIM] = vsel_ref[...]
    vs_ref[:, HEAD_DIM:] = ones_col
    vw_ref[:, :HEAD_DIM] = vwin_ref[...]
    vw_ref[:, HEAD_DIM:] = ones_col
    r_i = lax.broadcasted_iota(jnp.int32, (tq, tq), 0)
    c_i = lax.broadcasted_iota(jnp.int32, (tq, tq), 1)
    causal_bias = jnp.where(c_i <= r_i, 0.0, NEG_BIG)
    band_bias = jnp.where(c_i > r_i, 0.0, NEG_BIG)
    glane = lax.broadcasted_iota(jnp.int32, (tq, LANES), 1)
    for i in range(s // tq):
        q0 = i * tq
        q = q_ref[q0:q0 + tq, :]
        qa = jnp.concatenate([q, selb_ref[q0:q0 + tq, :]], axis=1)
        pieces = []
        if i > 0:
            pieces.append((lax.dot_general(qa, kaug_scr[0:q0, :], dn, preferred_element_type=F32),
                           vs_ref[0:q0, :]))
        pieces.append((lax.dot_general(qa, kaug_scr[q0:q0 + tq, :], dn, preferred_element_type=F32) + causal_bias,
                       vs_ref[q0:q0 + tq, :]))
        o_slc = _softmax_pv(pieces)
        pieces = []
        lo = q0 - WINDOW
        if lo >= 0:
            pieces.append((lax.dot_general(q, kw_ref[lo:lo + tq, :], dn, preferred_element_type=F32) + band_bias,
                           vw_ref[lo:lo + tq, :]))
        mid = max(lo + tq, 0)
        if q0 > mid:
            pieces.append((lax.dot_general(q, kw_ref[mid:q0, :], dn, preferred_element_type=F32),
                           vw_ref[mid:q0, :]))
        pieces.append((lax.dot_general(q, kw_ref[q0:q0 + tq, :], dn, preferred_element_type=F32) + causal_bias,
                       vw_ref[q0:q0 + tq, :]))
        o_win = _softmax_pv(pieces)
        gates = jax.nn.sigmoid(gl_ref[q0:q0 + tq, :])
        gsel = [jnp.sum(jnp.where(glane == hh * N_BRANCHES + br, gates, 0.0), axis=-1, keepdims=True)
                for br in range(N_BRANCHES)]
        o = gsel[0] * oc_ref[q0:q0 + tq, :].astype(F32) + gsel[1] * o_slc + gsel[2] * o_win
        o_ref[q0:q0 + tq, :] = o.astype(o_ref.dtype)


def _attention(q, selb, kv, o_cmp, gate_logits, batch):
    t = batch * SEQ
    g_ = N_KV_GROUPS
    head = lambda b, g, h: (b, g * HEADS_PER_GROUP + h)
    slot = lambda n: (lambda b, g, h: (b, n * g_ + g))
    return pl.pallas_call(
        _attn_body,
        grid=(batch, N_KV_GROUPS, HEADS_PER_GROUP),
        in_specs=[pl.BlockSpec((SEQ, HEAD_DIM), head),
                  pl.BlockSpec((SEQ, LANES), lambda b, g, h: (b * g_ + g, 0)),
                  pl.BlockSpec((SEQ, HEAD_DIM), slot(2)), pl.BlockSpec((SEQ, HEAD_DIM), slot(3)),
                  pl.BlockSpec((SEQ, HEAD_DIM), slot(4)), pl.BlockSpec((SEQ, HEAD_DIM), slot(5)),
                  pl.BlockSpec((SEQ, HEAD_DIM), head),
                  pl.BlockSpec((SEQ, LANES), lambda b, g, h: (b, 0))],
        out_specs=pl.BlockSpec((SEQ, HEAD_DIM), head),
        out_shape=jax.ShapeDtypeStruct((t, Q_WIDTH), BF16),
        scratch_shapes=[pltpu.VMEM((SEQ, 2 * HEAD_DIM), BF16)] * 3,
        compiler_params=_params(("arbitrary", "arbitrary", "arbitrary")),
        name="nsa_attn",
    )(q, selb, kv, kv, kv, kv, o_cmp, gate_logits)


def _nsa_mixer(hb, gate_logits, kv_w, w_in, w_out, cmp_k, cmp_v, batch):
    cos, sin = _rope_tables(jnp.arange(SEQ))
    kv = _matmul_rope(hb, kv_w.astype(BF16), cos, sin, BF16, "nsa_kv_proj", scale=1.0,
                      rope_cols=(2, 4), bm=MM_BM, bn=KV_SLOT_W)
    q = _matmul_rope(hb, w_in[:, :Q_WIDTH].astype(BF16), cos, sin, BF16, "nsa_q_proj",
                     scale=HEAD_DIM ** -0.5, rope_cols=None, bm=MM_BM, bn=MM_BN)

    def chunks(slot):
        c = kv[:, slot * KV_SLOT_W:(slot + 1) * KV_SLOT_W]
        c = c.reshape(batch, N_CHUNK, CMP_STRIDE, N_KV_GROUPS, HEAD_DIM).transpose(0, 3, 1, 2, 4)
        return c.reshape(batch * N_KV_GROUPS * N_CHUNK, CMP_STRIDE * HEAD_DIM)

    ccos, csin = _rope_tables(jnp.arange(N_CHUNK) * CMP_STRIDE + CMP_BLOCK - 1)
    k_cmp = _compress(chunks(0), *cmp_k, ccos, csin, True)
    v_cmp = _compress(chunks(1), *cmp_v, ccos, csin, False)
    o_cmp, selb = _cmp_attn(q, k_cmp, v_cmp, batch)
    o = _attention(q, selb, kv, o_cmp, gate_logits, batch)
    return _matmul(o, w_out.astype(BF16), F32, "nsa_out_proj")


def kernel(x, ln_mix_g, ln_mix_b, ln_ffn_g, ln_ffn_b, a_w_in, a_conv_w, a_conv_b, a_gate_a_w, a_gate_a_b,
           a_gate_x_w, a_gate_x_b, a_lambda, a_w_out, b_w_in, b_w_out, kv_w, cmp_k_pe, cmp_k_w1, cmp_k_b1,
           cmp_k_w2, cmp_k_b2, cmp_v_pe, cmp_v_w1, cmp_v_b1, cmp_v_w2, cmp_v_b2, moe_router_w, moe_router_b,
           moe_w_gu, moe_b_gu, moe_w_down, moe_b_down):
    batch = x.shape[0]
    t = batch * SEQ
    xf = x.reshape(t, D_MODEL)

    def moe(layer, res, mix, side=(), write_bf16=True):
        return _moe_block(res, mix, ln_mix_g[layer], ln_mix_b[layer], moe_router_w[layer], moe_router_b[layer],
                          _cast_layer_bf16(moe_w_gu, layer, "cast_w_gu"), moe_b_gu[layer],
                          _cast_layer_bf16(moe_w_down, layer, "cast_w_down"), moe_b_down[layer],
                          ln_ffn_g[layer], ln_ffn_b[layer], side, write_bf16)

    proj = _matmul(xf.astype(BF16), a_w_in[0].astype(BF16), BF16, "rg_in_proj")
    rec = _rglru(proj, a_conv_w[0], a_conv_b[0], a_gate_a_w[0].astype(BF16), a_gate_a_b[0],
                 a_gate_x_w[0].astype(BF16), a_gate_x_b[0], a_lambda[0], batch)
    mix = _matmul(rec, a_w_out[0].astype(BF16), F32, "rg_out_proj")
    n_gate = N_BRANCHES * N_HEADS
    gate_w = jnp.pad(b_w_in[0][:, Q_WIDTH:], ((0, 0), (0, LANES - n_gate)))
    h, hb, gate_logits = moe(0, xf, mix, side=((gate_w, jnp.zeros((LANES,), F32)),))

    mix = _nsa_mixer(hb, gate_logits, kv_w, b_w_in[0], b_w_out[0],
                     (cmp_k_pe, cmp_k_w1, cmp_k_b1, cmp_k_w2, cmp_k_b2),
                     (cmp_v_pe, cmp_v_w1, cmp_v_b1, cmp_v_w2, cmp_v_b2), batch)
    (h,) = moe(1, h, mix, write_bf16=False)
    return h.reshape(batch, SEQ, D_MODEL)
```

```python
import functools

import jax
import jax.numpy as jnp
from jax import lax
from jax.experimental import pallas as pl
from jax.experimental.pallas import tpu as pltpu

F32 = jnp.float32
BF16 = jnp.bfloat16

D_MODEL = 4096
SEQ = 2048
DEPTH = 2
DEEPNORM_ALPHA = (2 * DEPTH) ** 0.25
LN_EPS = 1e-5
RNN_WIDTH = D_MODEL
RG_BLOCKS = 16
RG_BLOCK_W = RNN_WIDTH // RG_BLOCKS
CONV_WIDTH = 4
RG_C = 8.0
N_HEADS = 32
HEAD_DIM = 128
N_KV_GROUPS = 4
HEADS_PER_GROUP = N_HEADS // N_KV_GROUPS
N_BRANCHES = 3
Q_WIDTH = N_HEADS * HEAD_DIM
KV_SLOT_W = N_KV_GROUPS * HEAD_DIM
KV_WIDTH = 2 * N_BRANCHES * KV_SLOT_W
CMP_BLOCK = 32
CMP_STRIDE = 16
CMP_HIDDEN = 512
N_CMP = (SEQ - CMP_BLOCK) // CMP_STRIDE + 1
N_CHUNK = SEQ // CMP_STRIDE
SLC_BLOCK = 64
N_SLC = SEQ // SLC_BLOCK
SLC_TOP_N = min(16, N_SLC)
WINDOW = 512
ROPE_THETA = 10000.0
NEG_BIG = -1e30
FORCE_BONUS = 1e4
N_EXPERTS = 32
TOP_K = 4
EXPERT_FF = D_MODEL // 8
SWIGLU_ALPHA = 1.702
SWIGLU_LIMIT = 7.0

LANES = 128
SUBLANES = 8
HALF_D = D_MODEL // 2
PACK_TILES = HALF_D // LANES
VMEM_LIMIT = 56 * 1024 * 1024

MM_BM = 1024
MM_BN = 1024
LN_ROWS = 256
COMBINE_ROWS = 128
RANK_ROWS = 512
MOE_TM = 512
FFN_OUT_COLS = 512
DMA_UNROLL = 4
ZERO_ROWS = 256
ATT_TQ = 256


def _params(sem, vmem=VMEM_LIMIT):
    return pltpu.CompilerParams(dimension_semantics=sem, vmem_limit_bytes=vmem)


def _mm_body(x_ref, w_ref, o_ref):
    o_ref[...] = jnp.dot(x_ref[...], w_ref[...], preferred_element_type=F32).astype(o_ref.dtype)


def _matmul(x, w, out_dtype, name, bm=MM_BM, bn=MM_BN):
    m, k = x.shape
    n = w.shape[1]
    return pl.pallas_call(
        _mm_body,
        grid=(n // bn, m // bm),
        in_specs=[pl.BlockSpec((bm, k), lambda j, i: (i, 0)),
                  pl.BlockSpec((k, bn), lambda j, i: (0, j))],
        out_specs=pl.BlockSpec((bm, bn), lambda j, i: (i, j)),
        out_shape=jax.ShapeDtypeStruct((m, n), out_dtype),
        compiler_params=_params(("arbitrary", "arbitrary")),
        name=name,
    )(x, w)


def _mm_rope_body(x_ref, w_ref, cos_ref, sin_ref, o_ref, *, scale, rope_cols):
    acc = jnp.dot(x_ref[...], w_ref[...], preferred_element_type=F32)
    cos = cos_ref[...]
    sin = sin_ref[...]
    if rope_cols is not None:
        j = pl.program_id(0)
        flag = functools.reduce(jnp.logical_or, [j == c for c in rope_cols])
        cos = jnp.where(flag, cos, 1.0)
        sin = jnp.where(flag, sin, 0.0)
    for h in range(acc.shape[1] // HEAD_DIM):
        a = acc[:, h * HEAD_DIM:(h + 1) * HEAD_DIM]
        r = a * cos + pltpu.roll(a, HEAD_DIM // 2, axis=1) * sin
        o_ref[:, h * HEAD_DIM:(h + 1) * HEAD_DIM] = (r * scale).astype(o_ref.dtype)


def _matmul_rope(x, w, cos, sin, out_dtype, name, *, scale, rope_cols, bm, bn):
    m, k = x.shape
    n = w.shape[1]
    s_blocks = SEQ // bm
    return pl.pallas_call(
        functools.partial(_mm_rope_body, scale=scale, rope_cols=rope_cols),
        grid=(n // bn, m // bm),
        in_specs=[pl.BlockSpec((bm, k), lambda j, i: (i, 0)),
                  pl.BlockSpec((k, bn), lambda j, i: (0, j)),
                  pl.BlockSpec((bm, HEAD_DIM), lambda j, i: (i % s_blocks, 0)),
                  pl.BlockSpec((bm, HEAD_DIM), lambda j, i: (i % s_blocks, 0))],
        out_specs=pl.BlockSpec((bm, bn), lambda j, i: (i, j)),
        out_shape=jax.ShapeDtypeStruct((m, n), out_dtype),
        compiler_params=_params(("arbitrary", "arbitrary")),
        name=name,
    )(x, w, cos, sin)


def _rope_tables(pos):
    half = HEAD_DIM // 2
    inv = ROPE_THETA ** (-jnp.arange(half, dtype=F32) / half)
    ang = pos.astype(F32)[:, None] * inv[None, :]
    cos = jnp.cos(ang)
    sin = jnp.sin(ang)
    return jnp.concatenate([cos, cos], axis=1), jnp.concatenate([-sin, sin], axis=1)


def _sigmoid(x):
    return 0.5 * jnp.tanh(0.5 * x) + 0.5


def _rglru_body(y_ref, u_ref, cw_ref, cb_ref, wa_ref, ba_ref, wx_ref, bx_ref, lam_ref, o_ref,
                a_scr, b_scr):
    s = u_ref.shape[0]
    u_in = u_ref[...].astype(F32)
    row = lax.broadcasted_iota(jnp.int32, u_in.shape, 0)
    u = u_in * cw_ref[CONV_WIDTH - 1:CONV_WIDTH, :] + cb_ref[...]
    for d in range(1, CONV_WIDTH):
        shifted = jnp.where(row >= d, pltpu.roll(u_in, d, axis=0), 0.0)
        u = u + shifted * cw_ref[CONV_WIDTH - 1 - d:CONV_WIDTH - d, :]
    ub = u.astype(BF16)
    gate_x = _sigmoid(jnp.dot(ub, wx_ref[0], preferred_element_type=F32) + bx_ref[0])
    gate_a = _sigmoid(jnp.dot(ub, wa_ref[0], preferred_element_type=F32) + ba_ref[0])
    z = -lam_ref[...]
    softplus = jnp.maximum(z, 0.0) + jnp.log1p(jnp.exp(-jnp.abs(z)))
    log_a = -RG_C * gate_a * softplus
    a = jnp.exp(log_a)
    b = jnp.sqrt(1.0 - a * a) * (gate_x * u)
    width = u_in.shape[1]
    a = a.reshape(s // SUBLANES, SUBLANES, width)
    b = b.reshape(s // SUBLANES, SUBLANES, width)
    sub = lax.broadcasted_iota(jnp.int32, a.shape, 1)
    for d in (1, 2, 4):
        valid = sub >= d
        a_prev = pltpu.roll(a, d, axis=1)
        b_prev = pltpu.roll(b, d, axis=1)
        b = jnp.where(valid, a * b_prev + b, b)
        a = jnp.where(valid, a * a_prev, a)
    a_scr[...] = a.reshape(s, width)
    b_scr[...] = b.reshape(s, width)

    def step(g, carry):
        r0 = pl.multiple_of(g * SUBLANES, SUBLANES)
        hs = a_scr[pl.ds(r0, SUBLANES), :] * carry + b_scr[pl.ds(r0, SUBLANES), :]
        b_scr[pl.ds(r0, SUBLANES), :] = hs
        return jnp.broadcast_to(hs[SUBLANES - 1:SUBLANES, :], hs.shape)

    lax.fori_loop(0, s // SUBLANES, step, jnp.zeros((SUBLANES, u_in.shape[1]), F32))
    y = jax.nn.gelu(y_ref[...].astype(F32))
    o_ref[...] = (b_scr[...] * y).astype(o_ref.dtype)


def _rglru(proj, conv_w, conv_b, gate_a_w, gate_a_b, gate_x_w, gate_x_b, lam, batch):
    w = RG_BLOCK_W
    t = batch * SEQ
    vec = lambda b, n: (0, n)
    blk = lambda b, n: (n, 0, 0)
    return pl.pallas_call(
        _rglru_body,
        grid=(batch, RG_BLOCKS),
        in_specs=[pl.BlockSpec((SEQ, w), lambda b, n: (b, n)),
                  pl.BlockSpec((SEQ, w), lambda b, n: (b, RG_BLOCKS + n)),
                  pl.BlockSpec((CONV_WIDTH, w), vec),
                  pl.BlockSpec((1, w), vec),
                  pl.BlockSpec((1, w, w), blk), pl.BlockSpec((1, 1, w), blk),
                  pl.BlockSpec((1, w, w), blk), pl.BlockSpec((1, 1, w), blk),
                  pl.BlockSpec((1, w), vec)],
        out_specs=pl.BlockSpec((SEQ, w), lambda b, n: (b, n)),
        out_shape=jax.ShapeDtypeStruct((t, RNN_WIDTH), BF16),
        scratch_shapes=[pltpu.VMEM((SEQ, w), F32), pltpu.VMEM((SEQ, w), F32)],
        compiler_params=_params(("arbitrary", "arbitrary")),
        name="rglru",
    )(proj, proj, conv_w, conv_b.reshape(1, -1),
      gate_a_w, gate_a_b.reshape(RG_BLOCKS, 1, w), gate_x_w, gate_x_b.reshape(RG_BLOCKS, 1, w),
      lam.reshape(1, -1))


def _split_hi_lo(w):
    hi = w.astype(BF16)
    lo = (w - hi.astype(F32)).astype(BF16)
    return hi, lo


def _dot3(x, whi_ref, wlo_ref):
    xh = x.astype(BF16)
    xl = (x - xh.astype(F32)).astype(BF16)
    whi = whi_ref[...]
    return (jnp.dot(xh, whi, preferred_element_type=F32)
            + jnp.dot(xl, whi, preferred_element_type=F32)
            + jnp.dot(xh, wlo_ref[...], preferred_element_type=F32))


def _layer_norm(z, g, b):
    mu = jnp.mean(z, axis=-1, keepdims=True)
    zc = z - mu
    var = jnp.mean(zc * zc, axis=-1, keepdims=True)
    return zc * lax.rsqrt(var + LN_EPS) * g + b


def _lane_pack(cols, lane):
    out = jnp.broadcast_to(cols[-1], lane.shape)
    for k in range(len(cols) - 2, -1, -1):
        out = jnp.where(lane == k, cols[k], out)
    return out


def _pack_pair(lo, hi):
    lo_bits = pltpu.bitcast(lo.astype(BF16).astype(F32), jnp.uint32) >> 16
    hi_bits = pltpu.bitcast(hi.astype(BF16).astype(F32), jnp.uint32) & jnp.uint32(0xFFFF0000)
    return lo_bits | hi_bits


def _unpack_pair(words):
    lo = pltpu.bitcast(words << 16, F32)
    hi = pltpu.bitcast(words & jnp.uint32(0xFFFF0000), F32)
    return lo, hi


def _rows_to_tiles(words):
    groups = jnp.stack([words[:, j * LANES:(j + 1) * LANES] for j in range(PACK_TILES)], axis=0)
    return jnp.swapaxes(groups, 0, 1)


def _tiles_to_rows(tiles):
    groups = jnp.swapaxes(tiles, 0, 1)
    return jnp.concatenate([groups[j] for j in range(PACK_TILES)], axis=1)


def _ln_router_body(res_ref, mix_ref, g_ref, b_ref, whi_ref, wlo_ref, rb_ref,
                    h_ref, hp_ref, ids_ref, wts_ref, cnt_ref):
    z = DEEPNORM_ALPHA * res_ref[...] + mix_ref[...].astype(F32)
    h = _layer_norm(z, g_ref[...], b_ref[...])
    h_ref[...] = h
    hp_ref[...] = _rows_to_tiles(_pack_pair(h[:, :HALF_D], h[:, HALF_D:]))
    logits = _dot3(h, whi_ref, wlo_ref) + rb_ref[...]
    lane = lax.broadcasted_iota(jnp.int32, logits.shape, 1)
    cur = jnp.where(lane < N_EXPERTS, logits, -jnp.inf)
    vals, idxs = [], []
    for _ in range(TOP_K):
        m = jnp.max(cur, axis=-1, keepdims=True)
        idx = jnp.min(jnp.where(cur == m, lane, LANES), axis=-1, keepdims=True)
        vals.append(m)
        idxs.append(idx)
        cur = jnp.where(lane == idx, -jnp.inf, cur)
    es = [jnp.exp(v - vals[0]) for v in vals]
    tot = functools.reduce(lambda p, q: p + q, es)
    ids_ref[...] = _lane_pack(idxs, lane)
    wts_ref[...] = _lane_pack([e / tot for e in es], lane)
    hot = functools.reduce(lambda p, q: p + q, [(lane == i).astype(F32) for i in idxs])

    @pl.when(pl.program_id(0) == 0)
    def _():
        cnt_ref[...] = jnp.zeros_like(cnt_ref)

    cnt_ref[...] += jnp.broadcast_to(jnp.sum(hot, axis=0, keepdims=True), cnt_ref.shape)


def _ln_router(res, mix, g, b, router_w, router_b):
    t = res.shape[0]
    rows = LN_ROWS
    w_pad = jnp.pad(router_w, ((0, 0), (0, LANES - N_EXPERTS)))
    whi, wlo = _split_hi_lo(w_pad)
    rb = jnp.pad(router_b, (0, LANES - N_EXPERTS)).reshape(1, LANES)
    const = lambda i: (0, 0)
    return pl.pallas_call(
        _ln_router_body,
        grid=(t // rows,),
        in_specs=[pl.BlockSpec((rows, D_MODEL), lambda i: (i, 0)),
                  pl.BlockSpec((rows, D_MODEL), lambda i: (i, 0)),
                  pl.BlockSpec((1, D_MODEL), const), pl.BlockSpec((1, D_MODEL), const),
                  pl.BlockSpec((D_MODEL, LANES), const), pl.BlockSpec((D_MODEL, LANES), const),
                  pl.BlockSpec((1, LANES), const)],
        out_specs=[pl.BlockSpec((rows, D_MODEL), lambda i: (i, 0)),
                   pl.BlockSpec((rows, PACK_TILES, LANES), lambda i: (i, 0, 0)),
                   pl.BlockSpec((rows, LANES), lambda i: (i, 0)),
                   pl.BlockSpec((rows, LANES), lambda i: (i, 0)),
                   pl.BlockSpec((SUBLANES, LANES), const)],
        out_shape=[jax.ShapeDtypeStruct((t, D_MODEL), F32),
                   jax.ShapeDtypeStruct((t, PACK_TILES, LANES), jnp.uint32),
                   jax.ShapeDtypeStruct((t, LANES), jnp.int32),
                   jax.ShapeDtypeStruct((t, LANES), F32),
                   jax.ShapeDtypeStruct((SUBLANES, LANES), F32)],
        compiler_params=_params(("arbitrary",)),
        name="ln_router",
    )(res, mix, g.reshape(1, -1), b.reshape(1, -1), whi, wlo, rb)


def _moe_tiles(t):
    return (t * TOP_K) // MOE_TM + N_EXPERTS


def _rank_body(ids_ref, cnt_ref, pos_ref, te_ref, meta_ref, carry_scr, base_scr):
    rows = ids_ref.shape[0]
    lane8 = lax.broadcasted_iota(jnp.int32, (SUBLANES, LANES), 1)

    @pl.when(pl.program_id(0) == 0)
    def _():
        cnt = cnt_ref[...]
        padded = jnp.floor((cnt + (MOE_TM - 1)) / MOE_TM) * MOE_TM
        padded = jnp.where(lane8 < N_EXPERTS, padded, 0.0)
        ends = padded
        d = 1
        while d < N_EXPERTS:
            ends = ends + jnp.where(lane8 >= d, pltpu.roll(ends, d, axis=1), 0.0)
            d *= 2
        base = ends - padded
        base_scr[...] = base
        carry_scr[...] = jnp.zeros_like(carry_scr)
        sub8 = lax.broadcasted_iota(jnp.int32, (SUBLANES, LANES), 0)
        meta = jnp.where(sub8 == 0, base, jnp.where(sub8 == 1, cnt, jnp.where(sub8 == 2, padded, ends)))
        meta_ref[...] = meta.astype(jnp.int32)
        tstart = lax.broadcasted_iota(jnp.int32, te_ref.shape, 0).astype(F32) * MOE_TM
        lane_t = lax.broadcasted_iota(jnp.int32, te_ref.shape, 1)
        done = jnp.where(lane_t < N_EXPERTS, (ends[0:1, :] <= tstart).astype(F32), 0.0)
        te = jnp.minimum(jnp.sum(done, axis=-1, keepdims=True), N_EXPERTS - 1.0)
        te_ref[...] = jnp.broadcast_to(te, te_ref.shape).astype(jnp.int32)

    ids = ids_ref[...]
    lane = lax.broadcasted_iota(jnp.int32, ids.shape, 1)
    hits = [ids[:, k:k + 1] == lane for k in range(TOP_K)]
    hot = functools.reduce(lambda p, q: p + q, [h.astype(F32) for h in hits])
    r_i = lax.broadcasted_iota(jnp.int32, (rows, rows), 0)
    c_i = lax.broadcasted_iota(jnp.int32, (rows, rows), 1)
    lower = jnp.where(r_i > c_i, 1.0, 0.0).astype(BF16)
    before = jnp.dot(lower, hot.astype(BF16), preferred_element_type=F32)
    slot = before + carry_scr[0:1, :] + base_scr[0:1, :]
    cols = [jnp.sum(jnp.where(h, slot, 0.0), axis=-1, keepdims=True) for h in hits]
    pos_ref[...] = _lane_pack(cols, lane).astype(jnp.int32)
    carry_scr[...] += jnp.broadcast_to(jnp.sum(hot, axis=0, keepdims=True), carry_scr.shape)


def _rank(ids, cnt):
    t = ids.shape[0]
    rows = RANK_ROWS
    nt_pad = -(-_moe_tiles(t) // SUBLANES) * SUBLANES
    const = lambda i: (0, 0)
    return pl.pallas_call(
        _rank_body,
        grid=(t // rows,),
        in_specs=[pl.BlockSpec((rows, LANES), lambda i: (i, 0)),
                  pl.BlockSpec((SUBLANES, LANES), const)],
        out_specs=[pl.BlockSpec((rows, LANES), lambda i: (i, 0)),
                   pl.BlockSpec((nt_pad, LANES), const),
                   pl.BlockSpec((SUBLANES, LANES), const)],
        out_shape=[jax.ShapeDtypeStruct((t, LANES), jnp.int32),
                   jax.ShapeDtypeStruct((nt_pad, LANES), jnp.int32),
                   jax.ShapeDtypeStruct((SUBLANES, LANES), jnp.int32)],
        scratch_shapes=[pltpu.VMEM((SUBLANES, LANES), F32), pltpu.VMEM((SUBLANES, LANES), F32)],
        compiler_params=_params(("arbitrary",)),
        name="moe_rank",
    )(ids, cnt)


def _ride_cast_specs(w, layer, n_steps):
    _, e, r, c = w.shape
    rb = (e * r) // n_steps
    assert rb * n_steps == e * r
    out_shape = jax.ShapeDtypeStruct((e, r, c), BF16)
    if rb > r:
        eb = rb // r
        assert eb * r == rb
        return (pl.BlockSpec((None, eb, r, c), lambda i, *_: (layer, i, 0, 0)),
                pl.BlockSpec((eb, r, c), lambda i, *_: (i, 0, 0)), out_shape)
    per = r // rb
    assert per * rb == r
    return (pl.BlockSpec((None, 1, rb, c), lambda i, *_: (layer, i // per, i % per, 0)),
            pl.BlockSpec((1, rb, c), lambda i, *_: (i // per, i % per, 0)), out_shape)


def _dispatch_body(pos_ref, base_ref, cnt_ref, pcnt_ref, h_ref, *rest, n_cast):
    cast_in, xs_ref, cast_out = rest[:n_cast], rest[n_cast], rest[n_cast + 1:2 * n_cast + 1]
    zero_scr, sem = rest[-2:]
    i = pl.program_id(0)
    rows = h_ref.shape[0]

    def row_copy(src_row_ref, slot):
        return pltpu.make_async_copy(src_row_ref, xs_ref.at[slot], sem)

    @pl.when(i == 0)
    def _():
        zero_scr[...] = jnp.zeros_like(zero_scr)
        zrows = zero_scr.shape[0]

        def zero_copy(first, size):
            return pltpu.make_async_copy(zero_scr.at[pl.ds(0, size)], xs_ref.at[pl.ds(first, size)], sem)

        def pad_copies(e, visit):
            first = base_ref[e] + cnt_ref[e]
            n = pcnt_ref[e] - cnt_ref[e]
            n_blocks = n // zrows

            def block(c, _):
                visit(zero_copy(first + c * zrows, zrows))
                return 0

            lax.fori_loop(0, n_blocks, block, 0)
            rem = n - n_blocks * zrows
            size = zrows // 2
            while size >= 1:
                @pl.when((rem & size) != 0)
                def _(size=size):
                    visit(zero_copy(first + n_blocks * zrows + (rem & ~(2 * size - 1)), size))

                size //= 2
            return 0

        lax.fori_loop(0, N_EXPERTS, lambda e, _: pad_copies(e, lambda cp: cp.start()), 0)
        lax.fori_loop(0, N_EXPERTS, lambda e, _: pad_copies(e, lambda cp: cp.wait()), 0)

    def start(t, _):
        for k in range(TOP_K):
            row_copy(h_ref.at[t], pos_ref[(i * rows + t) * TOP_K + k]).start(priority=k % 2)
        return 0

    lax.fori_loop(0, rows, start, 0, unroll=DMA_UNROLL)
    for w_in, w_out in zip(cast_in, cast_out):
        w_out[...] = w_in[...].astype(w_out.dtype)
    for _ in range(TOP_K):
        pltpu.make_async_copy(h_ref, xs_ref.at[pl.ds(0, rows)], sem).wait()


def _dispatch(h_packed, pos, base, cnt, pcnt, cast=()):
    t = h_packed.shape[0]
    rows = LN_ROWS
    n_steps = t // rows
    p_pad = _moe_tiles(t) * MOE_TM
    ride = [_ride_cast_specs(w, layer, n_steps) for w, layer in cast]
    outs = pl.pallas_call(
        functools.partial(_dispatch_body, n_cast=len(cast)),
        grid_spec=pltpu.PrefetchScalarGridSpec(
            num_scalar_prefetch=4,
            grid=(n_steps,),
            in_specs=[pl.BlockSpec((rows, PACK_TILES, LANES), lambda i, *_: (i, 0, 0))] + [r[0] for r in ride],
            out_specs=[pl.BlockSpec(memory_space=pl.ANY)] + [r[1] for r in ride],
            scratch_shapes=[pltpu.VMEM((ZERO_ROWS, PACK_TILES, LANES), jnp.uint32),
                            pltpu.SemaphoreType.DMA(())]),
        out_shape=[jax.ShapeDtypeStruct((p_pad, PACK_TILES, LANES), jnp.uint32)] + [r[2] for r in ride],
        compiler_params=_params(("arbitrary",)),
        name="moe_dispatch",
    )(pos, base, cnt, pcnt, h_packed, *[w for w, _ in cast])
    return outs[0], outs[1:]


def _ffn_body(te_ref, nt_ref, xs_ref, wgu_ref, bgu_ref, wd_ref, bd_ref, y_ref):
    @pl.when(pl.program_id(0) < nt_ref[0])
    def _():
        lo, hi = _unpack_pair(_tiles_to_rows(xs_ref[...]))
        x = jnp.concatenate([lo.astype(BF16), hi.astype(BF16)], axis=1)
        gu = jnp.dot(x, wgu_ref[0], preferred_element_type=F32) + bgu_ref[0]
        gate = jnp.minimum(gu[:, :EXPERT_FF], SWIGLU_LIMIT)
        up = jnp.clip(gu[:, EXPERT_FF:], -SWIGLU_LIMIT, SWIGLU_LIMIT)
        act = ((up + 1.0) * gate * _sigmoid(SWIGLU_ALPHA * gate)).astype(BF16)
        packed = []
        for c in range(0, HALF_D, FFN_OUT_COLS):
            d = c + HALF_D
            y_lo = jnp.dot(act, wd_ref[0, :, c:c + FFN_OUT_COLS], preferred_element_type=F32)
            y_hi = jnp.dot(act, wd_ref[0, :, d:d + FFN_OUT_COLS], preferred_element_type=F32)
            packed.append(_pack_pair(y_lo + bd_ref[0, :, c:c + FFN_OUT_COLS],
                                     y_hi + bd_ref[0, :, d:d + FFN_OUT_COLS]))
        y_ref[...] = _rows_to_tiles(jnp.concatenate(packed, axis=1))

    @pl.when(pl.program_id(0) >= nt_ref[0])
    def _():
        y_ref[...] = jnp.zeros_like(y_ref)


def _grouped_ffn(xs, tile_expert, n_tiles, w_gu, b_gu, w_down, b_down):
    p_pad = xs.shape[0]
    tm = MOE_TM
    row_blk = lambda i, te, nt: (jnp.minimum(i, nt[0] - 1), 0, 0)
    exp_blk = lambda i, te, nt: (te[i], 0, 0)
    return pl.pallas_call(
        _ffn_body,
        grid_spec=pltpu.PrefetchScalarGridSpec(
            num_scalar_prefetch=2,
            grid=(p_pad // tm,),
            in_specs=[pl.BlockSpec((tm, PACK_TILES, LANES), row_blk),
                      pl.BlockSpec((1, D_MODEL, 2 * EXPERT_FF), exp_blk),
                      pl.BlockSpec((1, 1, 2 * EXPERT_FF), exp_blk),
                      pl.BlockSpec((1, EXPERT_FF, D_MODEL), exp_blk),
                      pl.BlockSpec((1, 1, D_MODEL), exp_blk)],
            out_specs=pl.BlockSpec((tm, PACK_TILES, LANES), lambda i, te, nt: (i, 0, 0))),
        out_shape=jax.ShapeDtypeStruct(xs.shape, jnp.uint32),
        compiler_params=_params(("arbitrary",)),
        name="moe_ffn",
    )(tile_expert, n_tiles, xs, w_gu, b_gu.reshape(N_EXPERTS, 1, -1), w_down,
      b_down.reshape(N_EXPERTS, 1, -1))


def _combine_ln_body(pos_ref, res_ref, wts_ref, g_ref, b_ref, *rest, n_side, write_bf16, n_cast):
    side_refs = rest[:3 * n_side]
    y_hbm = rest[3 * n_side]
    cast_in = rest[3 * n_side + 1:3 * n_side + 1 + n_cast]
    outs = rest[3 * n_side + 1 + n_cast:-2 - n_cast]
    cast_out = rest[-2 - n_cast:-2]
    ybuf, sem = rest[-2:]
    i = pl.program_id(0)
    n_steps = pl.num_programs(0)
    rows = wts_ref.shape[0]
    cur = i % 2

    def row_copy(step, buf, t, k):
        slot = pos_ref[(step * rows + t) * TOP_K + k]
        return pltpu.make_async_copy(y_hbm.at[slot], ybuf.at[buf, k, t], sem.at[buf])

    def issue(step, buf):
        def body(t, _):
            for k in range(TOP_K):
                row_copy(step, buf, t, k).start(priority=k % 2)
            return 0

        lax.fori_loop(0, rows, body, 0, unroll=DMA_UNROLL)

    @pl.when(i == 0)
    def _():
        issue(0, 0)

    @pl.when(i + 1 < n_steps)
    def _():
        issue(i + 1, 1 - cur)

    for w_in, w_out in zip(cast_in, cast_out):
        w_out[...] = w_in[...].astype(w_out.dtype)

    for k in range(TOP_K):
        pltpu.make_async_copy(y_hbm.at[pl.ds(0, rows)], ybuf.at[cur, k], sem.at[cur]).wait()

    wts = wts_ref[...]
    res = res_ref[...]
    z_lo = DEEPNORM_ALPHA * res[:, :HALF_D]
    z_hi = DEEPNORM_ALPHA * res[:, HALF_D:]
    for k in range(TOP_K):
        lo, hi = _unpack_pair(_tiles_to_rows(ybuf[cur, k]))
        w_k = wts[:, k:k + 1]
        z_lo = z_lo + w_k * lo
        z_hi = z_hi + w_k * hi
    h = _layer_norm(jnp.concatenate([z_lo, z_hi], axis=1), g_ref[...], b_ref[...])
    outs[0][...] = h
    if write_bf16:
        outs[1][...] = h.astype(BF16)
    for s in range(n_side):
        whi_ref, wlo_ref, sb_ref = side_refs[3 * s:3 * s + 3]
        outs[1 + int(write_bf16) + s][...] = _dot3(h, whi_ref, wlo_ref) + sb_ref[...]


def _combine_ln(res, y_packed, pos, wts, g, b, side=(), write_bf16=True, cast=()):
    t = wts.shape[0]
    rows = COMBINE_ROWS
    const = lambda i, *_: (0, 0)
    side_args, side_specs = [], []
    for w_pad, bias in side:
        whi, wlo = _split_hi_lo(w_pad)
        side_args += [whi, wlo, bias.reshape(1, LANES)]
        side_specs += [pl.BlockSpec((D_MODEL, LANES), const), pl.BlockSpec((D_MODEL, LANES), const),
                       pl.BlockSpec((1, LANES), const)]
    n_side = len(side)
    n_wide = 1 + int(write_bf16)
    n_steps = t // rows
    ride = [_ride_cast_specs(w, layer, n_steps) for w, layer in cast]
    outs = pl.pallas_call(
        functools.partial(_combine_ln_body, n_side=n_side, write_bf16=write_bf16, n_cast=len(cast)),
        grid_spec=pltpu.PrefetchScalarGridSpec(
            num_scalar_prefetch=1,
            grid=(n_steps,),
            in_specs=[pl.BlockSpec((rows, D_MODEL), lambda i, *_: (i, 0)),
                      pl.BlockSpec((rows, LANES), lambda i, *_: (i, 0)),
                      pl.BlockSpec((1, D_MODEL), const), pl.BlockSpec((1, D_MODEL), const)]
            + side_specs + [pl.BlockSpec(memory_space=pl.ANY)] + [r[0] for r in ride],
            out_specs=[pl.BlockSpec((rows, D_MODEL), lambda i, *_: (i, 0))] * n_wide
            + [pl.BlockSpec((rows, LANES), lambda i, *_: (i, 0))] * n_side + [r[1] for r in ride],
            scratch_shapes=[pltpu.VMEM((2, TOP_K, rows, PACK_TILES, LANES), jnp.uint32),
                            pltpu.SemaphoreType.DMA((2,))]),
        out_shape=[jax.ShapeDtypeStruct((t, D_MODEL), F32), jax.ShapeDtypeStruct((t, D_MODEL), BF16)][:n_wide]
        + [jax.ShapeDtypeStruct((t, LANES), F32)] * n_side + [r[2] for r in ride],
        compiler_params=_params(("arbitrary",)),
        name="moe_combine_ln",
    )(pos, res, wts, g.reshape(1, -1), b.reshape(1, -1), *side_args, y_packed, *[w for w, _ in cast])
    n_own = n_wide + n_side
    return outs[:n_own], outs[n_own:]


def _moe_block(res, mix, ln_g, ln_b, router_w, router_b, weights, b_gu, b_down, fg, fb, side=(),
               write_bf16=True, next_cast=()):
    h, h_packed, ids, wts, cnt = _ln_router(res, mix, ln_g, ln_b, router_w, router_b)
    pos_pad, te_pad, meta = _rank(ids, cnt)
    pos = pos_pad[:, :TOP_K].reshape(-1)
    base, count, pcount = meta[0, :N_EXPERTS], meta[1, :N_EXPERTS], meta[2, :N_EXPERTS]
    n_tiles = (meta[3, N_EXPERTS - 1:N_EXPERTS] // MOE_TM).astype(jnp.int32)
    n_slots = _moe_tiles(res.shape[0]) * MOE_TM
    pcount = pcount.at[N_EXPERTS - 1].set(n_slots - base[N_EXPERTS - 1])
    if isinstance(weights[0], tuple):
        xs, (w_gu, w_down) = _dispatch(h_packed, pos, base, count, pcount, cast=weights)
    else:
        xs, _ = _dispatch(h_packed, pos, base, count, pcount)
        w_gu, w_down = weights
    y = _grouped_ffn(xs, te_pad[:, 0], n_tiles, w_gu, b_gu, w_down, b_down)
    return _combine_ln(h, y, pos, wts, fg, fb, side, write_bf16, cast=next_cast)


def _compress_body(c_ref, pet_ref, peb_ref, w1t_ref, w1b_ref, b1_ref, w2_ref, b2_ref, cos_ref, sin_ref,
                   o_ref, *, use_rope):
    rows = c_ref.shape[0]
    c = c_ref[...].astype(F32)
    top = jnp.dot((c + pet_ref[...]).astype(BF16), w1t_ref[...], preferred_element_type=F32)
    bot = jnp.dot((c + peb_ref[...]).astype(BF16), w1b_ref[...], preferred_element_type=F32)
    pre = top + pltpu.roll(bot, rows - 1, axis=0) + b1_ref[...]
    out = jnp.dot(jax.nn.gelu(pre).astype(BF16), w2_ref[...], preferred_element_type=F32) + b2_ref[...]
    if use_rope:
        out = out * cos_ref[...] + pltpu.roll(out, HEAD_DIM // 2, axis=1) * sin_ref[...]
    o_ref[...] = out.astype(o_ref.dtype)


def _compress(chunks, pe, w1, b1, w2, b2, cos, sin, use_rope):
    r = chunks.shape[0]
    rows = 4 * N_CHUNK
    half = CMP_STRIDE * HEAD_DIM
    const = lambda i: (0, 0)
    reps = rows // N_CHUNK
    return pl.pallas_call(
        functools.partial(_compress_body, use_rope=use_rope),
        grid=(r // rows,),
        in_specs=[pl.BlockSpec((rows, half), lambda i: (i, 0)),
                  pl.BlockSpec((1, half), const), pl.BlockSpec((1, half), const),
                  pl.BlockSpec((half, CMP_HIDDEN), const), pl.BlockSpec((half, CMP_HIDDEN), const),
                  pl.BlockSpec((1, CMP_HIDDEN), const),
                  pl.BlockSpec((CMP_HIDDEN, HEAD_DIM), const), pl.BlockSpec((1, HEAD_DIM), const),
                  pl.BlockSpec((rows, HEAD_DIM), const), pl.BlockSpec((rows, HEAD_DIM), const)],
        out_specs=pl.BlockSpec((rows, HEAD_DIM), lambda i: (i, 0)),
        out_shape=jax.ShapeDtypeStruct((r, HEAD_DIM), BF16),
        compiler_params=_params(("arbitrary",)),
        name="nsa_compress",
    )(chunks, pe[:CMP_STRIDE].reshape(1, half), pe[CMP_STRIDE:].reshape(1, half),
      w1[:half].astype(BF16), w1[half:].astype(BF16), b1.reshape(1, -1),
      w2.astype(BF16), b2.reshape(1, -1), jnp.tile(cos, (reps, 1)), jnp.tile(sin, (reps, 1)))


def _cmp_attn_body(q_ref, kc_ref, vc_ref, ovt_ref, o_ref, selb_ref):
    s = q_ref.shape[0]
    kc = kc_ref[...]
    vc = vc_ref[...]
    pos = lax.broadcasted_iota(jnp.int32, (s, N_CHUNK), 0)
    cidx = lax.broadcasted_iota(jnp.int32, (s, N_CHUNK), 1)
    mask = (cidx * CMP_STRIDE + (CMP_BLOCK - 1) <= pos) & (cidx < N_CMP)
    psum = jnp.zeros((s, N_CHUNK), F32)
    for h in range(HEADS_PER_GROUP):
        qh = q_ref[:, h * HEAD_DIM:(h + 1) * HEAD_DIM]
        sc = lax.dot_general(qh, kc, (((1,), (1,)), ((), ())), preferred_element_type=F32)
        sc = jnp.where(mask, sc, NEG_BIG)
        m = jnp.max(sc, axis=-1, keepdims=True)
        e = jnp.where(mask, jnp.exp(sc - m), 0.0)
        l = jnp.sum(e, axis=-1, keepdims=True)
        p = e / jnp.where(l > 0.0, l, 1.0)
        psum = psum + p
        o_ref[:, h * HEAD_DIM:(h + 1) * HEAD_DIM] = jnp.dot(
            p.astype(BF16), vc, preferred_element_type=F32).astype(o_ref.dtype)
    ph = psum.astype(BF16)
    pl_ = (psum - ph.astype(F32)).astype(BF16)
    ovt = ovt_ref[...]
    dn = (((1,), (1,)), ((), ()))
    imp = (lax.dot_general(ovt, ph, dn, preferred_element_type=F32)
           + lax.dot_general(ovt, pl_, dn, preferred_element_type=F32))[:N_SLC]
    blk = lax.broadcasted_iota(jnp.int32, (N_SLC, s), 0)
    qpos = lax.broadcasted_iota(jnp.int32, (N_SLC, s), 1)
    cur = qpos // SLC_BLOCK
    causal = blk * SLC_BLOCK <= qpos
    forced = (blk == 0) | (blk == cur) | (blk == cur - 1)
    val = jnp.where(causal, imp + jnp.where(forced, FORCE_BONUS, 0.0), -jnp.inf)
    rank = jnp.zeros((N_SLC, s), jnp.int32)
    for m_ in range(N_SLC):
        row = val[m_:m_ + 1, :]
        tie = jnp.where(blk > m_, 1, 0)
        rank = rank + jnp.where(row > val, 1, jnp.where(row == val, tie, 0))
    sel = causal & (rank < SLC_TOP_N)
    bias = jnp.where(sel, 0.0, NEG_BIG)
    bias = jnp.concatenate([bias, jnp.zeros((LANES - N_SLC, s), F32)], axis=0)
    selb_ref[...] = bias.T.astype(selb_ref.dtype)


def _cmp_attn(q, k_cmp, v_cmp, batch):
    t = batch * SEQ
    gw = HEADS_PER_GROUP * HEAD_DIM
    c0 = jnp.arange(N_CHUNK)[None, :] * CMP_STRIDE
    s0 = jnp.arange(LANES)[:, None] * SLC_BLOCK
    ovt = ((c0 < s0 + SLC_BLOCK) & (c0 + CMP_BLOCK > s0) & (jnp.arange(LANES)[:, None] < N_SLC)
           & (jnp.arange(N_CHUNK)[None, :] < N_CMP)).astype(BF16)
    return pl.pallas_call(
        _cmp_attn_body,
        grid=(batch, N_KV_GROUPS),
        in_specs=[pl.BlockSpec((SEQ, gw), lambda b, g: (b, g)),
                  pl.BlockSpec((N_CHUNK, HEAD_DIM), lambda b, g: (b * N_KV_GROUPS + g, 0)),
                  pl.BlockSpec((N_CHUNK, HEAD_DIM), lambda b, g: (b * N_KV_GROUPS + g, 0)),
                  pl.BlockSpec((LANES, N_CHUNK), lambda b, g: (0, 0))],
        out_specs=[pl.BlockSpec((SEQ, gw), lambda b, g: (b, g)),
                   pl.BlockSpec((SEQ, LANES), lambda b, g: (b * N_KV_GROUPS + g, 0))],
        out_shape=[jax.ShapeDtypeStruct((t, Q_WIDTH), BF16),
                   jax.ShapeDtypeStruct((batch * N_KV_GROUPS * SEQ, LANES), BF16)],
        compiler_params=_params(("arbitrary", "arbitrary")),
        name="nsa_cmp_attn",
    )(q, k_cmp, v_cmp, ovt)


def _softmax_pv(pieces):
    m = functools.reduce(jnp.maximum, [jnp.max(sc, axis=-1, keepdims=True) for sc, _ in pieces])
    o = 0.0
    for sc, v in pieces:
        e = jnp.exp((sc - m).astype(BF16))
        o = o + jnp.dot(e, v, preferred_element_type=F32)
    return o[:, :HEAD_DIM] / o[:, HEAD_DIM:HEAD_DIM + 1]


def _attn_body(q_ref, selb_ref, ks_ref, vsel_ref, kw_ref, vwin_ref, oc_ref, gl_ref, o_ref,
               kaug_scr, vs_ref, vw_ref):
    s = q_ref.shape[0]
    tq = ATT_TQ
    hh = pl.program_id(1) * HEADS_PER_GROUP + pl.program_id(2)
    dn = (((1,), (1,)), ((), ()))
    @pl.when(pl.program_id(2) == 0)
    def _():
        key = lax.broadcasted_iota(jnp.int32, (s, LANES), 0)
        lane = lax.broadcasted_iota(jnp.int32, (s, LANES), 1)
        kaug_scr[:, :HEAD_DIM] = ks_ref[...]
        kaug_scr[:, HEAD_DIM:] = jnp.where(key // SLC_BLOCK == lane, 1.0, 0.0).astype(BF16)
        ones_col = jnp.where(lane == 0, 1.0, 0.0).astype(BF16)
        vs_ref[:, :HEAD_DIM] = vsel_ref[...]
        vs_ref[:, HEAD_DIM:] = ones_col
        vw_ref[:, :HEAD_DIM] = vwin_ref[...]
        vw_ref[:, HEAD_DIM:] = ones_col


    r_i = lax.broadcasted_iota(jnp.int32, (tq, tq), 0)
    c_i = lax.broadcasted_iota(jnp.int32, (tq, tq), 1)
    causal_bias = jnp.where(c_i <= r_i, 0.0, NEG_BIG)
    band_bias = jnp.where(c_i > r_i, 0.0, NEG_BIG)
    glane = lax.broadcasted_iota(jnp.int32, (tq, LANES), 1)
    for i in range(s // tq):
        q0 = i * tq
        q = q_ref[q0:q0 + tq, :]
        qa = jnp.concatenate([q, selb_ref[q0:q0 + tq, :]], axis=1)
        pieces = []
        if i > 0:
            pieces.append((lax.dot_general(qa, kaug_scr[0:q0, :], dn, preferred_element_type=F32),
                           vs_ref[0:q0, :]))
        pieces.append((lax.dot_general(qa, kaug_scr[q0:q0 + tq, :], dn, preferred_element_type=F32) + causal_bias,
                       vs_ref[q0:q0 + tq, :]))
        o_slc = _softmax_pv(pieces)
        pieces = []
        lo = q0 - WINDOW
        if lo >= 0:
            pieces.append((lax.dot_general(q, kw_ref[lo:lo + tq, :], dn, preferred_element_type=F32) + band_bias,
                           vw_ref[lo:lo + tq, :]))
        mid = max(lo + tq, 0)
        if q0 > mid:
            pieces.append((lax.dot_general(q, kw_ref[mid:q0, :], dn, preferred_element_type=F32),
                           vw_ref[mid:q0, :]))
        pieces.append((lax.dot_general(q, kw_ref[q0:q0 + tq, :], dn, preferred_element_type=F32) + causal_bias,
                       vw_ref[q0:q0 + tq, :]))
        o_win = _softmax_pv(pieces)
        gates = jax.nn.sigmoid(gl_ref[q0:q0 + tq, :])
        gsel = [jnp.sum(jnp.where(glane == hh * N_BRANCHES + br, gates, 0.0), axis=-1, keepdims=True)
                for br in range(N_BRANCHES)]
        o = gsel[0] * oc_ref[q0:q0 + tq, :].astype(F32) + gsel[1] * o_slc + gsel[2] * o_win
        o_ref[q0:q0 + tq, :] = o.astype(o_ref.dtype)


def _attention(q, selb, kv, o_cmp, gate_logits, batch):
    t = batch * SEQ
    g_ = N_KV_GROUPS
    head = lambda b, g, h: (b, g * HEADS_PER_GROUP + h)
    slot = lambda n: (lambda b, g, h: (b, n * g_ + g))
    return pl.pallas_call(
        _attn_body,
        grid=(batch, N_KV_GROUPS, HEADS_PER_GROUP),
        in_specs=[pl.BlockSpec((SEQ, HEAD_DIM), head),
                  pl.BlockSpec((SEQ, LANES), lambda b, g, h: (b * g_ + g, 0)),
                  pl.BlockSpec((SEQ, HEAD_DIM), slot(2)), pl.BlockSpec((SEQ, HEAD_DIM), slot(3)),
                  pl.BlockSpec((SEQ, HEAD_DIM), slot(4)), pl.BlockSpec((SEQ, HEAD_DIM), slot(5)),
                  pl.BlockSpec((SEQ, HEAD_DIM), head),
                  pl.BlockSpec((SEQ, LANES), lambda b, g, h: (b, 0))],
        out_specs=pl.BlockSpec((SEQ, HEAD_DIM), head),
        out_shape=jax.ShapeDtypeStruct((t, Q_WIDTH), BF16),
        scratch_shapes=[pltpu.VMEM((SEQ, 2 * HEAD_DIM), BF16)] * 3,
        compiler_params=_params(("arbitrary", "arbitrary", "arbitrary")),
        name="nsa_attn",
    )(q, selb, kv, kv, kv, kv, o_cmp, gate_logits)


def _nsa_mixer(hb, gate_logits, kv_w, w_in, w_out, cmp_k, cmp_v, batch):
    cos, sin = _rope_tables(jnp.arange(SEQ))
    kv = _matmul_rope(hb, kv_w.astype(BF16), cos, sin, BF16, "nsa_kv_proj", scale=1.0,
                      rope_cols=(2, 4), bm=MM_BM, bn=KV_SLOT_W)
    q = _matmul_rope(hb, w_in[:, :Q_WIDTH].astype(BF16), cos, sin, BF16, "nsa_q_proj",
                     scale=HEAD_DIM ** -0.5, rope_cols=None, bm=MM_BM, bn=MM_BN)

    def chunks(slot):
        c = kv[:, slot * KV_SLOT_W:(slot + 1) * KV_SLOT_W]
        c = c.reshape(batch, N_CHUNK, CMP_STRIDE, N_KV_GROUPS, HEAD_DIM).transpose(0, 3, 1, 2, 4)
        return c.reshape(batch * N_KV_GROUPS * N_CHUNK, CMP_STRIDE * HEAD_DIM)

    ccos, csin = _rope_tables(jnp.arange(N_CHUNK) * CMP_STRIDE + CMP_BLOCK - 1)
    k_cmp = _compress(chunks(0), *cmp_k, ccos, csin, True)
    v_cmp = _compress(chunks(1), *cmp_v, ccos, csin, False)
    o_cmp, selb = _cmp_attn(q, k_cmp, v_cmp, batch)
    o = _attention(q, selb, kv, o_cmp, gate_logits, batch)
    return _matmul(o, w_out.astype(BF16), BF16, "nsa_out_proj")


def kernel(x, ln_mix_g, ln_mix_b, ln_ffn_g, ln_ffn_b, a_w_in, a_conv_w, a_conv_b, a_gate_a_w, a_gate_a_b,
           a_gate_x_w, a_gate_x_b, a_lambda, a_w_out, b_w_in, b_w_out, kv_w, cmp_k_pe, cmp_k_w1, cmp_k_b1,
           cmp_k_w2, cmp_k_b2, cmp_v_pe, cmp_v_w1, cmp_v_b1, cmp_v_w2, cmp_v_b2, moe_router_w, moe_router_b,
           moe_w_gu, moe_b_gu, moe_w_down, moe_b_down):
    batch = x.shape[0]
    t = batch * SEQ
    xf = x.reshape(t, D_MODEL)

    def moe(layer, res, mix, weights, **kw):
        return _moe_block(res, mix, ln_mix_g[layer], ln_mix_b[layer], moe_router_w[layer], moe_router_b[layer],
                          weights, moe_b_gu[layer], moe_b_down[layer], ln_ffn_g[layer], ln_ffn_b[layer], **kw)

    def expert_weights(layer):
        return ((moe_w_gu, layer), (moe_w_down, layer))

    proj = _matmul(xf.astype(BF16), a_w_in[0].astype(BF16), BF16, "rg_in_proj")
    rec = _rglru(proj, a_conv_w[0], a_conv_b[0], a_gate_a_w[0].astype(BF16), a_gate_a_b[0],
                 a_gate_x_w[0].astype(BF16), a_gate_x_b[0], a_lambda[0], batch)
    mix = _matmul(rec, a_w_out[0].astype(BF16), BF16, "rg_out_proj")
    n_gate = N_BRANCHES * N_HEADS
    gate_w = jnp.pad(b_w_in[0][:, Q_WIDTH:], ((0, 0), (0, LANES - n_gate)))
    (h, hb, gate_logits), next_weights = moe(0, xf, mix, expert_weights(0),
                                             side=((gate_w, jnp.zeros((LANES,), F32)),),
                                             next_cast=expert_weights(1))

    mix = _nsa_mixer(hb, gate_logits, kv_w, b_w_in[0], b_w_out[0],
                     (cmp_k_pe, cmp_k_w1, cmp_k_b1, cmp_k_w2, cmp_k_b2),
                     (cmp_v_pe, cmp_v_w1, cmp_v_b1, cmp_v_w2, cmp_v_b2), batch)
    (h,), _ = moe(1, h, mix, tuple(next_weights), write_bf16=False)
    return h.reshape(batch, SEQ, D_MODEL)
```

```python
import functools

import jax
import jax.numpy as jnp
from jax import lax
from jax.experimental import pallas as pl
from jax.experimental.pallas import tpu as pltpu

F32 = jnp.float32
BF16 = jnp.bfloat16

D_MODEL = 4096
SEQ = 2048
DEPTH = 2
DEEPNORM_ALPHA = (2 * DEPTH) ** 0.25
LN_EPS = 1e-5
RNN_WIDTH = D_MODEL
RG_BLOCKS = 16
RG_BLOCK_W = RNN_WIDTH // RG_BLOCKS
CONV_WIDTH = 4
RG_C = 8.0
N_HEADS = 32
HEAD_DIM = 128
N_KV_GROUPS = 4
HEADS_PER_GROUP = N_HEADS // N_KV_GROUPS
N_BRANCHES = 3
Q_WIDTH = N_HEADS * HEAD_DIM
KV_SLOT_W = N_KV_GROUPS * HEAD_DIM
KV_WIDTH = 2 * N_BRANCHES * KV_SLOT_W
CMP_BLOCK = 32
CMP_STRIDE = 16
CMP_HIDDEN = 512
N_CMP = (SEQ - CMP_BLOCK) // CMP_STRIDE + 1
N_CHUNK = SEQ // CMP_STRIDE
SLC_BLOCK = 64
N_SLC = SEQ // SLC_BLOCK
SLC_TOP_N = min(16, N_SLC)
WINDOW = 512
ROPE_THETA = 10000.0
NEG_BIG = -1e30
FORCE_BONUS = 1e4
N_EXPERTS = 32
TOP_K = 4
EXPERT_FF = D_MODEL // 8
SWIGLU_ALPHA = 1.702
SWIGLU_LIMIT = 7.0

LANES = 128
SUBLANES = 8
HALF_D = D_MODEL // 2
PACK_TILES = HALF_D // LANES
VMEM_LIMIT = 56 * 1024 * 1024

MM_BM = 1024
MM_BN = 1024
LN_ROWS = 256
COMBINE_ROWS = 128
RANK_ROWS = 512
MOE_TM = 512
FFN_OUT_COLS = 512
DMA_UNROLL = 4
ZERO_ROWS = 256
ATT_TQ = 256


def _params(sem, vmem=VMEM_LIMIT):
    return pltpu.CompilerParams(dimension_semantics=sem, vmem_limit_bytes=vmem)


def _mm_body(x_ref, w_ref, o_ref):
    o_ref[...] = jnp.dot(x_ref[...], w_ref[...], preferred_element_type=F32).astype(o_ref.dtype)


def _matmul(x, w, out_dtype, name, bm=MM_BM, bn=MM_BN):
    m, k = x.shape
    n = w.shape[1]
    return pl.pallas_call(
        _mm_body,
        grid=(n // bn, m // bm),
        in_specs=[pl.BlockSpec((bm, k), lambda j, i: (i, 0)),
                  pl.BlockSpec((k, bn), lambda j, i: (0, j))],
        out_specs=pl.BlockSpec((bm, bn), lambda j, i: (i, j)),
        out_shape=jax.ShapeDtypeStruct((m, n), out_dtype),
        compiler_params=_params(("arbitrary", "arbitrary")),
        name=name,
    )(x, w)


def _mm_rope_body(x_ref, w_ref, cos_ref, sin_ref, o_ref, *, scale, rope_cols):
    acc = jnp.dot(x_ref[...], w_ref[...], preferred_element_type=F32)
    cos = cos_ref[...]
    sin = sin_ref[...]
    if rope_cols is not None:
        j = pl.program_id(0)
        flag = functools.reduce(jnp.logical_or, [j == c for c in rope_cols])
        cos = jnp.where(flag, cos, 1.0)
        sin = jnp.where(flag, sin, 0.0)
    for h in range(acc.shape[1] // HEAD_DIM):
        a = acc[:, h * HEAD_DIM:(h + 1) * HEAD_DIM]
        r = a * cos + pltpu.roll(a, HEAD_DIM // 2, axis=1) * sin
        o_ref[:, h * HEAD_DIM:(h + 1) * HEAD_DIM] = (r * scale).astype(o_ref.dtype)


def _matmul_rope(x, w, cos, sin, out_dtype, name, *, scale, rope_cols, bm, bn):
    m, k = x.shape
    n = w.shape[1]
    s_blocks = SEQ // bm
    return pl.pallas_call(
        functools.partial(_mm_rope_body, scale=scale, rope_cols=rope_cols),
        grid=(n // bn, m // bm),
        in_specs=[pl.BlockSpec((bm, k), lambda j, i: (i, 0)),
                  pl.BlockSpec((k, bn), lambda j, i: (0, j)),
                  pl.BlockSpec((bm, HEAD_DIM), lambda j, i: (i % s_blocks, 0)),
                  pl.BlockSpec((bm, HEAD_DIM), lambda j, i: (i % s_blocks, 0))],
        out_specs=pl.BlockSpec((bm, bn), lambda j, i: (i, j)),
        out_shape=jax.ShapeDtypeStruct((m, n), out_dtype),
        compiler_params=_params(("arbitrary", "arbitrary")),
        name=name,
    )(x, w, cos, sin)


def _rope_tables(pos):
    half = HEAD_DIM // 2
    inv = ROPE_THETA ** (-jnp.arange(half, dtype=F32) / half)
    ang = pos.astype(F32)[:, None] * inv[None, :]
    cos = jnp.cos(ang)
    sin = jnp.sin(ang)
    return jnp.concatenate([cos, cos], axis=1), jnp.concatenate([-sin, sin], axis=1)


def _sigmoid(x):
    return 0.5 * jnp.tanh(0.5 * x) + 0.5


def _rglru_body(y_ref, u_ref, cw_ref, cb_ref, wa_ref, ba_ref, wx_ref, bx_ref, lam_ref, *rest, n_cast):
    o_ref = rest[n_cast]
    a_scr, b_scr = rest[-2:]
    _ride_casts(rest[:n_cast], rest[n_cast + 1:2 * n_cast + 1])
    s = u_ref.shape[0]
    u_in = u_ref[...].astype(F32)
    row = lax.broadcasted_iota(jnp.int32, u_in.shape, 0)
    u = u_in * cw_ref[CONV_WIDTH - 1:CONV_WIDTH, :] + cb_ref[...]
    for d in range(1, CONV_WIDTH):
        shifted = jnp.where(row >= d, pltpu.roll(u_in, d, axis=0), 0.0)
        u = u + shifted * cw_ref[CONV_WIDTH - 1 - d:CONV_WIDTH - d, :]
    ub = u.astype(BF16)
    gate_x = _sigmoid(jnp.dot(ub, wx_ref[0], preferred_element_type=F32) + bx_ref[0])
    gate_a = _sigmoid(jnp.dot(ub, wa_ref[0], preferred_element_type=F32) + ba_ref[0])
    z = -lam_ref[...]
    softplus = jnp.maximum(z, 0.0) + jnp.log1p(jnp.exp(-jnp.abs(z)))
    log_a = -RG_C * gate_a * softplus
    a = jnp.exp(log_a)
    b = jnp.sqrt(1.0 - a * a) * (gate_x * u)
    width = u_in.shape[1]
    a = a.reshape(s // SUBLANES, SUBLANES, width)
    b = b.reshape(s // SUBLANES, SUBLANES, width)
    sub = lax.broadcasted_iota(jnp.int32, a.shape, 1)
    for d in (1, 2, 4):
        valid = sub >= d
        a_prev = pltpu.roll(a, d, axis=1)
        b_prev = pltpu.roll(b, d, axis=1)
        b = jnp.where(valid, a * b_prev + b, b)
        a = jnp.where(valid, a * a_prev, a)
    a_scr[...] = a.reshape(s, width)
    b_scr[...] = b.reshape(s, width)

    def step(g, carry):
        r0 = pl.multiple_of(g * SUBLANES, SUBLANES)
        hs = a_scr[pl.ds(r0, SUBLANES), :] * carry + b_scr[pl.ds(r0, SUBLANES), :]
        b_scr[pl.ds(r0, SUBLANES), :] = hs
        return jnp.broadcast_to(hs[SUBLANES - 1:SUBLANES, :], hs.shape)

    lax.fori_loop(0, s // SUBLANES, step, jnp.zeros((SUBLANES, u_in.shape[1]), F32))
    y = jax.nn.gelu(y_ref[...].astype(F32))
    o_ref[...] = (b_scr[...] * y).astype(o_ref.dtype)


def _rglru(proj, conv_w, conv_b, gate_a_w, gate_a_b, gate_x_w, gate_x_b, lam, batch, cast=()):
    w = RG_BLOCK_W
    t = batch * SEQ
    vec = lambda b, n: (0, n)
    blk = lambda b, n: (n, 0, 0)
    ride = [_ride_cast_specs(cw, layer, batch * RG_BLOCKS, lambda b, n: b * RG_BLOCKS + n) for cw, layer in cast]
    outs = pl.pallas_call(
        functools.partial(_rglru_body, n_cast=len(cast)),
        grid=(batch, RG_BLOCKS),
        in_specs=[pl.BlockSpec((SEQ, w), lambda b, n: (b, n)),
                  pl.BlockSpec((SEQ, w), lambda b, n: (b, RG_BLOCKS + n)),
                  pl.BlockSpec((CONV_WIDTH, w), vec),
                  pl.BlockSpec((1, w), vec),
                  pl.BlockSpec((1, w, w), blk), pl.BlockSpec((1, 1, w), blk),
                  pl.BlockSpec((1, w, w), blk), pl.BlockSpec((1, 1, w), blk),
                  pl.BlockSpec((1, w), vec)] + [r[0] for r in ride],
        out_specs=[pl.BlockSpec((SEQ, w), lambda b, n: (b, n))] + [r[1] for r in ride],
        out_shape=[jax.ShapeDtypeStruct((t, RNN_WIDTH), BF16)] + [r[2] for r in ride],
        scratch_shapes=[pltpu.VMEM((SEQ, w), F32), pltpu.VMEM((SEQ, w), F32)],
        compiler_params=_params(("arbitrary", "arbitrary")),
        name="rglru",
    )(proj, proj, conv_w, conv_b.reshape(1, -1),
      gate_a_w, gate_a_b.reshape(RG_BLOCKS, 1, w), gate_x_w, gate_x_b.reshape(RG_BLOCKS, 1, w),
      lam.reshape(1, -1), *[cw for cw, _ in cast])
    return outs[0], tuple(outs[1:])


def _split_hi_lo(w):
    hi = w.astype(BF16)
    lo = (w - hi.astype(F32)).astype(BF16)
    return hi, lo


def _dot3(x, whi_ref, wlo_ref):
    xh = x.astype(BF16)
    xl = (x - xh.astype(F32)).astype(BF16)
    whi = whi_ref[...]
    return (jnp.dot(xh, whi, preferred_element_type=F32)
            + jnp.dot(xl, whi, preferred_element_type=F32)
            + jnp.dot(xh, wlo_ref[...], preferred_element_type=F32))


def _layer_norm(z, g, b):
    mu = jnp.mean(z, axis=-1, keepdims=True)
    zc = z - mu
    var = jnp.mean(zc * zc, axis=-1, keepdims=True)
    return zc * lax.rsqrt(var + LN_EPS) * g + b


def _lane_pack(cols, lane):
    out = jnp.broadcast_to(cols[-1], lane.shape)
    for k in range(len(cols) - 2, -1, -1):
        out = jnp.where(lane == k, cols[k], out)
    return out


def _pack_pair(lo, hi):
    lo_bits = pltpu.bitcast(lo.astype(BF16).astype(F32), jnp.uint32) >> 16
    hi_bits = pltpu.bitcast(hi.astype(BF16).astype(F32), jnp.uint32) & jnp.uint32(0xFFFF0000)
    return lo_bits | hi_bits


def _unpack_pair(words):
    lo = pltpu.bitcast(words << 16, F32)
    hi = pltpu.bitcast(words & jnp.uint32(0xFFFF0000), F32)
    return lo, hi


def _rows_to_tiles(words):
    groups = jnp.stack([words[:, j * LANES:(j + 1) * LANES] for j in range(PACK_TILES)], axis=0)
    return jnp.swapaxes(groups, 0, 1)


def _tiles_to_rows(tiles):
    groups = jnp.swapaxes(tiles, 0, 1)
    return jnp.concatenate([groups[j] for j in range(PACK_TILES)], axis=1)


def _ln_router_body(res_ref, mix_ref, g_ref, b_ref, whi_ref, wlo_ref, rb_ref,
                    h_ref, hp_ref, ids_ref, wts_ref, cnt_ref):
    z = DEEPNORM_ALPHA * res_ref[...] + mix_ref[...].astype(F32)
    h = _layer_norm(z, g_ref[...], b_ref[...])
    h_ref[...] = h
    hp_ref[...] = _rows_to_tiles(_pack_pair(h[:, :HALF_D], h[:, HALF_D:]))
    logits = _dot3(h, whi_ref, wlo_ref) + rb_ref[...]
    lane = lax.broadcasted_iota(jnp.int32, logits.shape, 1)
    cur = jnp.where(lane < N_EXPERTS, logits, -jnp.inf)
    vals, idxs = [], []
    for _ in range(TOP_K):
        m = jnp.max(cur, axis=-1, keepdims=True)
        idx = jnp.min(jnp.where(cur == m, lane, LANES), axis=-1, keepdims=True)
        vals.append(m)
        idxs.append(idx)
        cur = jnp.where(lane == idx, -jnp.inf, cur)
    es = [jnp.exp(v - vals[0]) for v in vals]
    tot = functools.reduce(lambda p, q: p + q, es)
    ids_ref[...] = _lane_pack(idxs, lane)
    wts_ref[...] = _lane_pack([e / tot for e in es], lane)
    hot = functools.reduce(lambda p, q: p + q, [(lane == i).astype(F32) for i in idxs])

    @pl.when(pl.program_id(0) == 0)
    def _():
        cnt_ref[...] = jnp.zeros_like(cnt_ref)

    cnt_ref[...] += jnp.broadcast_to(jnp.sum(hot, axis=0, keepdims=True), cnt_ref.shape)


def _ln_router(res, mix, g, b, router_w, router_b):
    t = res.shape[0]
    rows = LN_ROWS
    w_pad = jnp.pad(router_w, ((0, 0), (0, LANES - N_EXPERTS)))
    whi, wlo = _split_hi_lo(w_pad)
    rb = jnp.pad(router_b, (0, LANES - N_EXPERTS)).reshape(1, LANES)
    const = lambda i: (0, 0)
    return pl.pallas_call(
        _ln_router_body,
        grid=(t // rows,),
        in_specs=[pl.BlockSpec((rows, D_MODEL), lambda i: (i, 0)),
                  pl.BlockSpec((rows, D_MODEL), lambda i: (i, 0)),
                  pl.BlockSpec((1, D_MODEL), const), pl.BlockSpec((1, D_MODEL), const),
                  pl.BlockSpec((D_MODEL, LANES), const), pl.BlockSpec((D_MODEL, LANES), const),
                  pl.BlockSpec((1, LANES), const)],
        out_specs=[pl.BlockSpec((rows, D_MODEL), lambda i: (i, 0)),
                   pl.BlockSpec((rows, PACK_TILES, LANES), lambda i: (i, 0, 0)),
                   pl.BlockSpec((rows, LANES), lambda i: (i, 0)),
                   pl.BlockSpec((rows, LANES), lambda i: (i, 0)),
                   pl.BlockSpec((SUBLANES, LANES), const)],
        out_shape=[jax.ShapeDtypeStruct((t, D_MODEL), F32),
                   jax.ShapeDtypeStruct((t, PACK_TILES, LANES), jnp.uint32),
                   jax.ShapeDtypeStruct((t, LANES), jnp.int32),
                   jax.ShapeDtypeStruct((t, LANES), F32),
                   jax.ShapeDtypeStruct((SUBLANES, LANES), F32)],
        compiler_params=_params(("arbitrary",)),
        name="ln_router",
    )(res, mix, g.reshape(1, -1), b.reshape(1, -1), whi, wlo, rb)


def _moe_tiles(t):
    return (t * TOP_K) // MOE_TM + N_EXPERTS


def _rank_body(ids_ref, cnt_ref, pos_ref, te_ref, meta_ref, carry_scr, base_scr):
    rows = ids_ref.shape[0]
    lane8 = lax.broadcasted_iota(jnp.int32, (SUBLANES, LANES), 1)

    @pl.when(pl.program_id(0) == 0)
    def _():
        cnt = cnt_ref[...]
        padded = jnp.floor((cnt + (MOE_TM - 1)) / MOE_TM) * MOE_TM
        padded = jnp.where(lane8 < N_EXPERTS, padded, 0.0)
        ends = padded
        d = 1
        while d < N_EXPERTS:
            ends = ends + jnp.where(lane8 >= d, pltpu.roll(ends, d, axis=1), 0.0)
            d *= 2
        base = ends - padded
        base_scr[...] = base
        carry_scr[...] = jnp.zeros_like(carry_scr)
        sub8 = lax.broadcasted_iota(jnp.int32, (SUBLANES, LANES), 0)
        meta = jnp.where(sub8 == 0, base, jnp.where(sub8 == 1, cnt, jnp.where(sub8 == 2, padded, ends)))
        meta_ref[...] = meta.astype(jnp.int32)
        tstart = lax.broadcasted_iota(jnp.int32, te_ref.shape, 0).astype(F32) * MOE_TM
        lane_t = lax.broadcasted_iota(jnp.int32, te_ref.shape, 1)
        done = jnp.where(lane_t < N_EXPERTS, (ends[0:1, :] <= tstart).astype(F32), 0.0)
        te = jnp.minimum(jnp.sum(done, axis=-1, keepdims=True), N_EXPERTS - 1.0)
        te_ref[...] = jnp.broadcast_to(te, te_ref.shape).astype(jnp.int32)

    ids = ids_ref[...]
    lane = lax.broadcasted_iota(jnp.int32, ids.shape, 1)
    hits = [ids[:, k:k + 1] == lane for k in range(TOP_K)]
    hot = functools.reduce(lambda p, q: p + q, [h.astype(F32) for h in hits])
    r_i = lax.broadcasted_iota(jnp.int32, (rows, rows), 0)
    c_i = lax.broadcasted_iota(jnp.int32, (rows, rows), 1)
    lower = jnp.where(r_i > c_i, 1.0, 0.0).astype(BF16)
    before = jnp.dot(lower, hot.astype(BF16), preferred_element_type=F32)
    slot = before + carry_scr[0:1, :] + base_scr[0:1, :]
    cols = [jnp.sum(jnp.where(h, slot, 0.0), axis=-1, keepdims=True) for h in hits]
    pos_ref[...] = _lane_pack(cols, lane).astype(jnp.int32)
    carry_scr[...] += jnp.broadcast_to(jnp.sum(hot, axis=0, keepdims=True), carry_scr.shape)


def _rank(ids, cnt):
    t = ids.shape[0]
    rows = RANK_ROWS
    nt_pad = -(-_moe_tiles(t) // SUBLANES) * SUBLANES
    const = lambda i: (0, 0)
    return pl.pallas_call(
        _rank_body,
        grid=(t // rows,),
        in_specs=[pl.BlockSpec((rows, LANES), lambda i: (i, 0)),
                  pl.BlockSpec((SUBLANES, LANES), const)],
        out_specs=[pl.BlockSpec((rows, LANES), lambda i: (i, 0)),
                   pl.BlockSpec((nt_pad, LANES), const),
                   pl.BlockSpec((SUBLANES, LANES), const)],
        out_shape=[jax.ShapeDtypeStruct((t, LANES), jnp.int32),
                   jax.ShapeDtypeStruct((nt_pad, LANES), jnp.int32),
                   jax.ShapeDtypeStruct((SUBLANES, LANES), jnp.int32)],
        scratch_shapes=[pltpu.VMEM((SUBLANES, LANES), F32), pltpu.VMEM((SUBLANES, LANES), F32)],
        compiler_params=_params(("arbitrary",)),
        name="moe_rank",
    )(ids, cnt)


def _ride_cast_specs(w, layer, n_steps, step_of=lambda i, *_: i):
    _, e, r, c = w.shape
    rb = (e * r) // n_steps
    assert rb * n_steps == e * r
    out_shape = jax.ShapeDtypeStruct((e, r, c), BF16)
    if rb > r:
        eb = rb // r
        assert eb * r == rb
        return (pl.BlockSpec((None, eb, r, c), lambda *a: (layer, step_of(*a), 0, 0)),
                pl.BlockSpec((eb, r, c), lambda *a: (step_of(*a), 0, 0)), out_shape)
    per = r // rb
    assert per * rb == r
    return (pl.BlockSpec((None, 1, rb, c), lambda *a: (layer, step_of(*a) // per, step_of(*a) % per, 0)),
            pl.BlockSpec((1, rb, c), lambda *a: (step_of(*a) // per, step_of(*a) % per, 0)), out_shape)


def _ride_casts(cast_in, cast_out):
    for w_in, w_out in zip(cast_in, cast_out):
        w_out[...] = w_in[...].astype(w_out.dtype)


def _dispatch_body(pos_ref, base_ref, cnt_ref, pcnt_ref, h_ref, *rest, n_cast):
    cast_in, xs_ref, cast_out = rest[:n_cast], rest[n_cast], rest[n_cast + 1:2 * n_cast + 1]
    zero_scr, sem = rest[-2:]
    i = pl.program_id(0)
    rows = h_ref.shape[0]

    def row_copy(src_row_ref, slot):
        return pltpu.make_async_copy(src_row_ref, xs_ref.at[slot], sem)

    @pl.when(i == 0)
    def _():
        zero_scr[...] = jnp.zeros_like(zero_scr)
        zrows = zero_scr.shape[0]

        def zero_copy(first, size):
            return pltpu.make_async_copy(zero_scr.at[pl.ds(0, size)], xs_ref.at[pl.ds(first, size)], sem)

        def pad_copies(e, visit):
            first = base_ref[e] + cnt_ref[e]
            n = pcnt_ref[e] - cnt_ref[e]
            n_blocks = n // zrows

            def block(c, _):
                visit(zero_copy(first + c * zrows, zrows))
                return 0

            lax.fori_loop(0, n_blocks, block, 0)
            rem = n - n_blocks * zrows
            size = zrows // 2
            while size >= 1:
                @pl.when((rem & size) != 0)
                def _(size=size):
                    visit(zero_copy(first + n_blocks * zrows + (rem & ~(2 * size - 1)), size))

                size //= 2
            return 0

        lax.fori_loop(0, N_EXPERTS, lambda e, _: pad_copies(e, lambda cp: cp.start()), 0)
        lax.fori_loop(0, N_EXPERTS, lambda e, _: pad_copies(e, lambda cp: cp.wait()), 0)

    def start(t, _):
        for k in range(TOP_K):
            row_copy(h_ref.at[t], pos_ref[(i * rows + t) * TOP_K + k]).start(priority=k % 2)
        return 0

    lax.fori_loop(0, rows, start, 0, unroll=DMA_UNROLL)
    for w_in, w_out in zip(cast_in, cast_out):
        w_out[...] = w_in[...].astype(w_out.dtype)
    for _ in range(TOP_K):
        pltpu.make_async_copy(h_ref, xs_ref.at[pl.ds(0, rows)], sem).wait()


def _dispatch(h_packed, pos, base, cnt, pcnt, cast=()):
    t = h_packed.shape[0]
    rows = LN_ROWS
    n_steps = t // rows
    p_pad = _moe_tiles(t) * MOE_TM
    ride = [_ride_cast_specs(w, layer, n_steps) for w, layer in cast]
    outs = pl.pallas_call(
        functools.partial(_dispatch_body, n_cast=len(cast)),
        grid_spec=pltpu.PrefetchScalarGridSpec(
            num_scalar_prefetch=4,
            grid=(n_steps,),
            in_specs=[pl.BlockSpec((rows, PACK_TILES, LANES), lambda i, *_: (i, 0, 0))] + [r[0] for r in ride],
            out_specs=[pl.BlockSpec(memory_space=pl.ANY)] + [r[1] for r in ride],
            scratch_shapes=[pltpu.VMEM((ZERO_ROWS, PACK_TILES, LANES), jnp.uint32),
                            pltpu.SemaphoreType.DMA(())]),
        out_shape=[jax.ShapeDtypeStruct((p_pad, PACK_TILES, LANES), jnp.uint32)] + [r[2] for r in ride],
        compiler_params=_params(("arbitrary",)),
        name="moe_dispatch",
    )(pos, base, cnt, pcnt, h_packed, *[w for w, _ in cast])
    return outs[0], outs[1:]


def _ffn_body(te_ref, nt_ref, xs_ref, wgu_ref, bgu_ref, wd_ref, bd_ref, y_ref):
    @pl.when(pl.program_id(0) < nt_ref[0])
    def _():
        lo, hi = _unpack_pair(_tiles_to_rows(xs_ref[...]))
        x = jnp.concatenate([lo.astype(BF16), hi.astype(BF16)], axis=1)
        gu = jnp.dot(x, wgu_ref[0], preferred_element_type=F32) + bgu_ref[0]
        gate = jnp.minimum(gu[:, :EXPERT_FF], SWIGLU_LIMIT)
        up = jnp.clip(gu[:, EXPERT_FF:], -SWIGLU_LIMIT, SWIGLU_LIMIT)
        act = ((up + 1.0) * gate * _sigmoid(SWIGLU_ALPHA * gate)).astype(BF16)
        packed = []
        for c in range(0, HALF_D, FFN_OUT_COLS):
            d = c + HALF_D
            y_lo = jnp.dot(act, wd_ref[0, :, c:c + FFN_OUT_COLS], preferred_element_type=F32)
            y_hi = jnp.dot(act, wd_ref[0, :, d:d + FFN_OUT_COLS], preferred_element_type=F32)
            packed.append(_pack_pair(y_lo + bd_ref[0, :, c:c + FFN_OUT_COLS],
                                     y_hi + bd_ref[0, :, d:d + FFN_OUT_COLS]))
        y_ref[...] = _rows_to_tiles(jnp.concatenate(packed, axis=1))

    @pl.when(pl.program_id(0) >= nt_ref[0])
    def _():
        y_ref[...] = jnp.zeros_like(y_ref)


def _grouped_ffn(xs, tile_expert, n_tiles, w_gu, b_gu, w_down, b_down):
    p_pad = xs.shape[0]
    tm = MOE_TM
    row_blk = lambda i, te, nt: (jnp.minimum(i, nt[0] - 1), 0, 0)
    exp_blk = lambda i, te, nt: (te[i], 0, 0)
    return pl.pallas_call(
        _ffn_body,
        grid_spec=pltpu.PrefetchScalarGridSpec(
            num_scalar_prefetch=2,
            grid=(p_pad // tm,),
            in_specs=[pl.BlockSpec((tm, PACK_TILES, LANES), row_blk),
                      pl.BlockSpec((1, D_MODEL, 2 * EXPERT_FF), exp_blk),
                      pl.BlockSpec((1, 1, 2 * EXPERT_FF), exp_blk),
                      pl.BlockSpec((1, EXPERT_FF, D_MODEL), exp_blk),
                      pl.BlockSpec((1, 1, D_MODEL), exp_blk)],
            out_specs=pl.BlockSpec((tm, PACK_TILES, LANES), lambda i, te, nt: (i, 0, 0))),
        out_shape=jax.ShapeDtypeStruct(xs.shape, jnp.uint32),
        compiler_params=_params(("arbitrary",)),
        name="moe_ffn",
    )(tile_expert, n_tiles, xs, w_gu, b_gu.reshape(N_EXPERTS, 1, -1), w_down,
      b_down.reshape(N_EXPERTS, 1, -1))


def _combine_ln_body(pos_ref, res_ref, wts_ref, g_ref, b_ref, *rest, n_side, write_bf16, n_cast):
    side_refs = rest[:3 * n_side]
    y_hbm = rest[3 * n_side]
    cast_in = rest[3 * n_side + 1:3 * n_side + 1 + n_cast]
    outs = rest[3 * n_side + 1 + n_cast:-2 - n_cast]
    cast_out = rest[-2 - n_cast:-2]
    ybuf, sem = rest[-2:]
    i = pl.program_id(0)
    n_steps = pl.num_programs(0)
    rows = wts_ref.shape[0]
    cur = i % 2

    def row_copy(step, buf, t, k):
        slot = pos_ref[(step * rows + t) * TOP_K + k]
        return pltpu.make_async_copy(y_hbm.at[slot], ybuf.at[buf, k, t], sem.at[buf])

    def issue(step, buf):
        def body(t, _):
            for k in range(TOP_K):
                row_copy(step, buf, t, k).start(priority=k % 2)
            return 0

        lax.fori_loop(0, rows, body, 0, unroll=DMA_UNROLL)

    @pl.when(i == 0)
    def _():
        issue(0, 0)

    @pl.when(i + 1 < n_steps)
    def _():
        issue(i + 1, 1 - cur)

    for w_in, w_out in zip(cast_in, cast_out):
        w_out[...] = w_in[...].astype(w_out.dtype)

    for k in range(TOP_K):
        pltpu.make_async_copy(y_hbm.at[pl.ds(0, rows)], ybuf.at[cur, k], sem.at[cur]).wait()

    wts = wts_ref[...]
    res = res_ref[...]
    z_lo = DEEPNORM_ALPHA * res[:, :HALF_D]
    z_hi = DEEPNORM_ALPHA * res[:, HALF_D:]
    for k in range(TOP_K):
        lo, hi = _unpack_pair(_tiles_to_rows(ybuf[cur, k]))
        w_k = wts[:, k:k + 1]
        z_lo = z_lo + w_k * lo
        z_hi = z_hi + w_k * hi
    h = _layer_norm(jnp.concatenate([z_lo, z_hi], axis=1), g_ref[...], b_ref[...])
    outs[0][...] = h
    if write_bf16:
        outs[1][...] = h.astype(BF16)
    for s in range(n_side):
        whi_ref, wlo_ref, sb_ref = side_refs[3 * s:3 * s + 3]
        outs[1 + int(write_bf16) + s][...] = _dot3(h, whi_ref, wlo_ref) + sb_ref[...]


def _combine_ln(res, y_packed, pos, wts, g, b, side=(), write_bf16=True, cast=()):
    t = wts.shape[0]
    rows = COMBINE_ROWS
    const = lambda i, *_: (0, 0)
    side_args, side_specs = [], []
    for w_pad, bias in side:
        whi, wlo = _split_hi_lo(w_pad)
        side_args += [whi, wlo, bias.reshape(1, LANES)]
        side_specs += [pl.BlockSpec((D_MODEL, LANES), const), pl.BlockSpec((D_MODEL, LANES), const),
                       pl.BlockSpec((1, LANES), const)]
    n_side = len(side)
    n_wide = 1 + int(write_bf16)
    n_steps = t // rows
    ride = [_ride_cast_specs(w, layer, n_steps) for w, layer in cast]
    outs = pl.pallas_call(
        functools.partial(_combine_ln_body, n_side=n_side, write_bf16=write_bf16, n_cast=len(cast)),
        grid_spec=pltpu.PrefetchScalarGridSpec(
            num_scalar_prefetch=1,
            grid=(n_steps,),
            in_specs=[pl.BlockSpec((rows, D_MODEL), lambda i, *_: (i, 0)),
                      pl.BlockSpec((rows, LANES), lambda i, *_: (i, 0)),
                      pl.BlockSpec((1, D_MODEL), const), pl.BlockSpec((1, D_MODEL), const)]
            + side_specs + [pl.BlockSpec(memory_space=pl.ANY)] + [r[0] for r in ride],
            out_specs=[pl.BlockSpec((rows, D_MODEL), lambda i, *_: (i, 0))] * n_wide
            + [pl.BlockSpec((rows, LANES), lambda i, *_: (i, 0))] * n_side + [r[1] for r in ride],
            scratch_shapes=[pltpu.VMEM((2, TOP_K, rows, PACK_TILES, LANES), jnp.uint32),
                            pltpu.SemaphoreType.DMA((2,))]),
        out_shape=[jax.ShapeDtypeStruct((t, D_MODEL), F32), jax.ShapeDtypeStruct((t, D_MODEL), BF16)][:n_wide]
        + [jax.ShapeDtypeStruct((t, LANES), F32)] * n_side + [r[2] for r in ride],
        compiler_params=_params(("arbitrary",)),
        name="moe_combine_ln",
    )(pos, res, wts, g.reshape(1, -1), b.reshape(1, -1), *side_args, y_packed, *[w for w, _ in cast])
    n_own = n_wide + n_side
    return outs[:n_own], outs[n_own:]


def _moe_block(res, mix, ln_g, ln_b, router_w, router_b, weights, b_gu, b_down, fg, fb, side=(),
               write_bf16=True, next_cast=()):
    h, h_packed, ids, wts, cnt = _ln_router(res, mix, ln_g, ln_b, router_w, router_b)
    pos_pad, te_pad, meta = _rank(ids, cnt)
    pos = pos_pad[:, :TOP_K].reshape(-1)
    base, count, pcount = meta[0, :N_EXPERTS], meta[1, :N_EXPERTS], meta[2, :N_EXPERTS]
    n_tiles = (meta[3, N_EXPERTS - 1:N_EXPERTS] // MOE_TM).astype(jnp.int32)
    n_slots = _moe_tiles(res.shape[0]) * MOE_TM
    pcount = pcount.at[N_EXPERTS - 1].set(n_slots - base[N_EXPERTS - 1])
    if isinstance(weights[0], tuple):
        xs, (w_gu, w_down) = _dispatch(h_packed, pos, base, count, pcount, cast=weights)
    else:
        xs, _ = _dispatch(h_packed, pos, base, count, pcount)
        w_gu, w_down = weights
    y = _grouped_ffn(xs, te_pad[:, 0], n_tiles, w_gu, b_gu, w_down, b_down)
    return _combine_ln(h, y, pos, wts, fg, fb, side, write_bf16, cast=next_cast)


def _compress_body(c_ref, pet_ref, peb_ref, w1t_ref, w1b_ref, b1_ref, w2_ref, b2_ref, cos_ref, sin_ref,
                   o_ref, *, use_rope):
    rows = c_ref.shape[0]
    c = c_ref[...].astype(F32)
    top = jnp.dot((c + pet_ref[...]).astype(BF16), w1t_ref[...], preferred_element_type=F32)
    bot = jnp.dot((c + peb_ref[...]).astype(BF16), w1b_ref[...], preferred_element_type=F32)
    pre = top + pltpu.roll(bot, rows - 1, axis=0) + b1_ref[...]
    out = jnp.dot(jax.nn.gelu(pre).astype(BF16), w2_ref[...], preferred_element_type=F32) + b2_ref[...]
    if use_rope:
        out = out * cos_ref[...] + pltpu.roll(out, HEAD_DIM // 2, axis=1) * sin_ref[...]
    o_ref[...] = out.astype(o_ref.dtype)


def _compress(chunks, pe, w1, b1, w2, b2, cos, sin, use_rope):
    r = chunks.shape[0]
    rows = 4 * N_CHUNK
    half = CMP_STRIDE * HEAD_DIM
    const = lambda i: (0, 0)
    reps = rows // N_CHUNK
    return pl.pallas_call(
        functools.partial(_compress_body, use_rope=use_rope),
        grid=(r // rows,),
        in_specs=[pl.BlockSpec((rows, half), lambda i: (i, 0)),
                  pl.BlockSpec((1, half), const), pl.BlockSpec((1, half), const),
                  pl.BlockSpec((half, CMP_HIDDEN), const), pl.BlockSpec((half, CMP_HIDDEN), const),
                  pl.BlockSpec((1, CMP_HIDDEN), const),
                  pl.BlockSpec((CMP_HIDDEN, HEAD_DIM), const), pl.BlockSpec((1, HEAD_DIM), const),
                  pl.BlockSpec((rows, HEAD_DIM), const), pl.BlockSpec((rows, HEAD_DIM), const)],
        out_specs=pl.BlockSpec((rows, HEAD_DIM), lambda i: (i, 0)),
        out_shape=jax.ShapeDtypeStruct((r, HEAD_DIM), BF16),
        compiler_params=_params(("arbitrary",)),
        name="nsa_compress",
    )(chunks, pe[:CMP_STRIDE].reshape(1, half), pe[CMP_STRIDE:].reshape(1, half),
      w1[:half].astype(BF16), w1[half:].astype(BF16), b1.reshape(1, -1),
      w2.astype(BF16), b2.reshape(1, -1), jnp.tile(cos, (reps, 1)), jnp.tile(sin, (reps, 1)))


def _cmp_attn_body(q_ref, kc_ref, vc_ref, ovt_ref, o_ref, selb_ref):
    s = q_ref.shape[0]
    kc = kc_ref[...]
    vc = vc_ref[...]
    pos = lax.broadcasted_iota(jnp.int32, (s, N_CHUNK), 0)
    cidx = lax.broadcasted_iota(jnp.int32, (s, N_CHUNK), 1)
    mask = (cidx * CMP_STRIDE + (CMP_BLOCK - 1) <= pos) & (cidx < N_CMP)
    psum = jnp.zeros((s, N_CHUNK), F32)
    for h in range(HEADS_PER_GROUP):
        qh = q_ref[:, h * HEAD_DIM:(h + 1) * HEAD_DIM]
        sc = lax.dot_general(qh, kc, (((1,), (1,)), ((), ())), preferred_element_type=F32)
        sc = jnp.where(mask, sc, NEG_BIG)
        m = jnp.max(sc, axis=-1, keepdims=True)
        e = jnp.where(mask, jnp.exp(sc - m), 0.0)
        l = jnp.sum(e, axis=-1, keepdims=True)
        p = e / jnp.where(l > 0.0, l, 1.0)
        psum = psum + p
        o_ref[:, h * HEAD_DIM:(h + 1) * HEAD_DIM] = jnp.dot(
            p.astype(BF16), vc, preferred_element_type=F32).astype(o_ref.dtype)
    ph = psum.astype(BF16)
    pl_ = (psum - ph.astype(F32)).astype(BF16)
    ovt = ovt_ref[...]
    dn = (((1,), (1,)), ((), ()))
    imp = (lax.dot_general(ovt, ph, dn, preferred_element_type=F32)
           + lax.dot_general(ovt, pl_, dn, preferred_element_type=F32))[:N_SLC]
    blk = lax.broadcasted_iota(jnp.int32, (N_SLC, s), 0)
    qpos = lax.broadcasted_iota(jnp.int32, (N_SLC, s), 1)
    cur = qpos // SLC_BLOCK
    causal = blk * SLC_BLOCK <= qpos
    forced = (blk == 0) | (blk == cur) | (blk == cur - 1)
    val = jnp.where(causal, imp + jnp.where(forced, FORCE_BONUS, 0.0), -jnp.inf)
    rank = jnp.zeros((N_SLC, s), jnp.int32)
    for m_ in range(N_SLC):
        row = val[m_:m_ + 1, :]
        tie = jnp.where(blk > m_, 1, 0)
        rank = rank + jnp.where(row > val, 1, jnp.where(row == val, tie, 0))
    sel = causal & (rank < SLC_TOP_N)
    bias = jnp.where(sel, 0.0, NEG_BIG)
    bias = jnp.concatenate([bias, jnp.zeros((LANES - N_SLC, s), F32)], axis=0)
    selb_ref[...] = bias.T.astype(selb_ref.dtype)


def _cmp_attn(q, k_cmp, v_cmp, batch):
    t = batch * SEQ
    gw = HEADS_PER_GROUP * HEAD_DIM
    c0 = jnp.arange(N_CHUNK)[None, :] * CMP_STRIDE
    s0 = jnp.arange(LANES)[:, None] * SLC_BLOCK
    ovt = ((c0 < s0 + SLC_BLOCK) & (c0 + CMP_BLOCK > s0) & (jnp.arange(LANES)[:, None] < N_SLC)
           & (jnp.arange(N_CHUNK)[None, :] < N_CMP)).astype(BF16)
    return pl.pallas_call(
        _cmp_attn_body,
        grid=(batch, N_KV_GROUPS),
        in_specs=[pl.BlockSpec((SEQ, gw), lambda b, g: (b, g)),
                  pl.BlockSpec((N_CHUNK, HEAD_DIM), lambda b, g: (b * N_KV_GROUPS + g, 0)),
                  pl.BlockSpec((N_CHUNK, HEAD_DIM), lambda b, g: (b * N_KV_GROUPS + g, 0)),
                  pl.BlockSpec((LANES, N_CHUNK), lambda b, g: (0, 0))],
        out_specs=[pl.BlockSpec((SEQ, gw), lambda b, g: (b, g)),
                   pl.BlockSpec((SEQ, LANES), lambda b, g: (b * N_KV_GROUPS + g, 0))],
        out_shape=[jax.ShapeDtypeStruct((t, Q_WIDTH), BF16),
                   jax.ShapeDtypeStruct((batch * N_KV_GROUPS * SEQ, LANES), BF16)],
        compiler_params=_params(("arbitrary", "arbitrary")),
        name="nsa_cmp_attn",
    )(q, k_cmp, v_cmp, ovt)


def _softmax_pv(pieces):
    m = functools.reduce(jnp.maximum, [jnp.max(sc, axis=-1, keepdims=True) for sc, _ in pieces])
    o = 0.0
    for sc, v in pieces:
        e = jnp.exp((sc - m).astype(BF16))
        o = o + jnp.dot(e, v, preferred_element_type=F32)
    return o[:, :HEAD_DIM] / o[:, HEAD_DIM:HEAD_DIM + 1]


def _attn_body(q_ref, selb_ref, ks_ref, vsel_ref, kw_ref, vwin_ref, oc_ref, gl_ref, *rest, n_cast):
    o_ref = rest[n_cast]
    kaug_scr, vs_ref, vw_ref = rest[-3:]
    _ride_casts(rest[:n_cast], rest[n_cast + 1:2 * n_cast + 1])
    s = q_ref.shape[0]
    tq = ATT_TQ
    hh = pl.program_id(1) * HEADS_PER_GROUP + pl.program_id(2)
    dn = (((1,), (1,)), ((), ()))
    @pl.when(pl.program_id(2) == 0)
    def _():
        key = lax.broadcasted_iota(jnp.int32, (s, LANES), 0)
        lane = lax.broadcasted_iota(jnp.int32, (s, LANES), 1)
        kaug_scr[:, :HEAD_DIM] = ks_ref[...]
        kaug_scr[:, HEAD_DIM:] = jnp.where(key // SLC_BLOCK == lane, 1.0, 0.0).astype(BF16)
        ones_col = jnp.where(lane == 0, 1.0, 0.0).astype(BF16)
        vs_ref[:, :HEAD_DIM] = vsel_ref[...]
        vs_ref[:, HEAD_DIM:] = ones_col
        vw_ref[:, :HEAD_DIM] = vwin_ref[...]
        vw_ref[:, HEAD_DIM:] = ones_col


    r_i = lax.broadcasted_iota(jnp.int32, (tq, tq), 0)
    c_i = lax.broadcasted_iota(jnp.int32, (tq, tq), 1)
    causal_bias = jnp.where(c_i <= r_i, 0.0, NEG_BIG)
    band_bias = jnp.where(c_i > r_i, 0.0, NEG_BIG)
    glane = lax.broadcasted_iota(jnp.int32, (tq, LANES), 1)
    for i in range(s // tq):
        q0 = i * tq
        q = q_ref[q0:q0 + tq, :]
        qa = jnp.concatenate([q, selb_ref[q0:q0 + tq, :]], axis=1)
        pieces = []
        if i > 0:
            pieces.append((lax.dot_general(qa, kaug_scr[0:q0, :], dn, preferred_element_type=F32),
                           vs_ref[0:q0, :]))
        pieces.append((lax.dot_general(qa, kaug_scr[q0:q0 + tq, :], dn, preferred_element_type=F32) + causal_bias,
                       vs_ref[q0:q0 + tq, :]))
        o_slc = _softmax_pv(pieces)
        pieces = []
        lo = q0 - WINDOW
        if lo >= 0:
            pieces.append((lax.dot_general(q, kw_ref[lo:lo + tq, :], dn, preferred_element_type=F32) + band_bias,
                           vw_ref[lo:lo + tq, :]))
        mid = max(lo + tq, 0)
        if q0 > mid:
            pieces.append((lax.dot_general(q, kw_ref[mid:q0, :], dn, preferred_element_type=F32),
                           vw_ref[mid:q0, :]))
        pieces.append((lax.dot_general(q, kw_ref[q0:q0 + tq, :], dn, preferred_element_type=F32) + causal_bias,
                       vw_ref[q0:q0 + tq, :]))
        o_win = _softmax_pv(pieces)
        gates = jax.nn.sigmoid(gl_ref[q0:q0 + tq, :])
        gsel = [jnp.sum(jnp.where(glane == hh * N_BRANCHES + br, gates, 0.0), axis=-1, keepdims=True)
                for br in range(N_BRANCHES)]
        o = gsel[0] * oc_ref[q0:q0 + tq, :].astype(F32) + gsel[1] * o_slc + gsel[2] * o_win
        o_ref[q0:q0 + tq, :] = o.astype(o_ref.dtype)


def _attention(q, selb, kv, o_cmp, gate_logits, batch, cast=()):
    t = batch * SEQ
    g_ = N_KV_GROUPS
    head = lambda b, g, h: (b, g * HEADS_PER_GROUP + h)
    slot = lambda n: (lambda b, g, h: (b, n * g_ + g))
    ride = [_ride_cast_specs(cw, layer, batch * N_HEADS, lambda b, g, h: (b * g_ + g) * HEADS_PER_GROUP + h)
            for cw, layer in cast]
    outs = pl.pallas_call(
        functools.partial(_attn_body, n_cast=len(cast)),
        grid=(batch, N_KV_GROUPS, HEADS_PER_GROUP),
        in_specs=[pl.BlockSpec((SEQ, HEAD_DIM), head),
                  pl.BlockSpec((SEQ, LANES), lambda b, g, h: (b * g_ + g, 0)),
                  pl.BlockSpec((SEQ, HEAD_DIM), slot(2)), pl.BlockSpec((SEQ, HEAD_DIM), slot(3)),
                  pl.BlockSpec((SEQ, HEAD_DIM), slot(4)), pl.BlockSpec((SEQ, HEAD_DIM), slot(5)),
                  pl.BlockSpec((SEQ, HEAD_DIM), head),
                  pl.BlockSpec((SEQ, LANES), lambda b, g, h: (b, 0))] + [r[0] for r in ride],
        out_specs=[pl.BlockSpec((SEQ, HEAD_DIM), head)] + [r[1] for r in ride],
        out_shape=[jax.ShapeDtypeStruct((t, Q_WIDTH), BF16)] + [r[2] for r in ride],
        scratch_shapes=[pltpu.VMEM((SEQ, 2 * HEAD_DIM), BF16)] * 3,
        compiler_params=_params(("arbitrary", "arbitrary", "arbitrary")),
        name="nsa_attn",
    )(q, selb, kv, kv, kv, kv, o_cmp, gate_logits, *[cw for cw, _ in cast])
    return outs[0], tuple(outs[1:])


def _nsa_mixer(hb, gate_logits, kv_w, w_in, w_out, cmp_k, cmp_v, batch, cast=()):
    cos, sin = _rope_tables(jnp.arange(SEQ))
    kv = _matmul_rope(hb, kv_w.astype(BF16), cos, sin, BF16, "nsa_kv_proj", scale=1.0,
                      rope_cols=(2, 4), bm=MM_BM, bn=KV_SLOT_W)
    q = _matmul_rope(hb, w_in[:, :Q_WIDTH].astype(BF16), cos, sin, BF16, "nsa_q_proj",
                     scale=HEAD_DIM ** -0.5, rope_cols=None, bm=MM_BM, bn=MM_BN)

    def chunks(slot):
        c = kv[:, slot * KV_SLOT_W:(slot + 1) * KV_SLOT_W]
        c = c.reshape(batch, N_CHUNK, CMP_STRIDE, N_KV_GROUPS, HEAD_DIM).transpose(0, 3, 1, 2, 4)
        return c.reshape(batch * N_KV_GROUPS * N_CHUNK, CMP_STRIDE * HEAD_DIM)

    ccos, csin = _rope_tables(jnp.arange(N_CHUNK) * CMP_STRIDE + CMP_BLOCK - 1)
    k_cmp = _compress(chunks(0), *cmp_k, ccos, csin, True)
    v_cmp = _compress(chunks(1), *cmp_v, ccos, csin, False)
    o_cmp, selb = _cmp_attn(q, k_cmp, v_cmp, batch)
    o, converted = _attention(q, selb, kv, o_cmp, gate_logits, batch, cast)
    return _matmul(o, w_out.astype(BF16), BF16, "nsa_out_proj"), converted


def kernel(x, ln_mix_g, ln_mix_b, ln_ffn_g, ln_ffn_b, a_w_in, a_conv_w, a_conv_b, a_gate_a_w, a_gate_a_b,
           a_gate_x_w, a_gate_x_b, a_lambda, a_w_out, b_w_in, b_w_out, kv_w, cmp_k_pe, cmp_k_w1, cmp_k_b1,
           cmp_k_w2, cmp_k_b2, cmp_v_pe, cmp_v_w1, cmp_v_b1, cmp_v_w2, cmp_v_b2, moe_router_w, moe_router_b,
           moe_w_gu, moe_b_gu, moe_w_down, moe_b_down):
    batch = x.shape[0]
    t = batch * SEQ
    xf = x.reshape(t, D_MODEL)

    def moe(layer, res, mix, weights, **kw):
        return _moe_block(res, mix, ln_mix_g[layer], ln_mix_b[layer], moe_router_w[layer], moe_router_b[layer],
                          weights, moe_b_gu[layer], moe_b_down[layer], ln_ffn_g[layer], ln_ffn_b[layer], **kw)

    def expert_weights(layer):
        return ((moe_w_gu, layer), (moe_w_down, layer))

    proj = _matmul(xf.astype(BF16), a_w_in[0].astype(BF16), BF16, "rg_in_proj")
    rec, weights0 = _rglru(proj, a_conv_w[0], a_conv_b[0], a_gate_a_w[0].astype(BF16), a_gate_a_b[0],
                           a_gate_x_w[0].astype(BF16), a_gate_x_b[0], a_lambda[0], batch, expert_weights(0))
    mix = _matmul(rec, a_w_out[0].astype(BF16), BF16, "rg_out_proj")
    n_gate = N_BRANCHES * N_HEADS
    gate_w = jnp.pad(b_w_in[0][:, Q_WIDTH:], ((0, 0), (0, LANES - n_gate)))
    (h, hb, gate_logits), _ = moe(0, xf, mix, weights0, side=((gate_w, jnp.zeros((LANES,), F32)),))

    mix, weights1 = _nsa_mixer(hb, gate_logits, kv_w, b_w_in[0], b_w_out[0],
                               (cmp_k_pe, cmp_k_w1, cmp_k_b1, cmp_k_w2, cmp_k_b2),
                               (cmp_v_pe, cmp_v_w1, cmp_v_b1, cmp_v_w2, cmp_v_b2), batch, expert_weights(1))
    (h,), _ = moe(1, h, mix, weights1, write_bf16=False)
    return h.reshape(batch, SEQ, D_MODEL)
```
